```python
import jax, jax.numpy as jnp
from jax import lax
import numpy as np

D_MODEL = 1024
BATCH = 8
SEQ = 4096
DEPTH = 2

N_MIXERS = 2
RWKV_HEAD_SIZE = 64
RWKV_HEADS = D_MODEL // RWKV_HEAD_SIZE
DECAY_LORA = 64
AAA_LORA = 64
GATE_LORA = 160
GN_EPS = 64e-5
N_SHIFT_BRANCHES = 6
POOL_WINDOWS = (2, 4, 8, 16)
POOL_GROUPS = 4
POOL_GROUP_WIDTH = D_MODEL // POOL_GROUPS
PEER_HEADS = 8
N_KEYS = 128
N_EXPERTS = N_KEYS * N_KEYS
PEER_TOPK = 16
PEER_DK = 256
PEER_DK_HALF = PEER_DK // 2
PEER_BLOCK = 128
ALPHA = (2 * DEPTH) ** 0.25
BETA = (8 * DEPTH) ** -0.25
LN_EPS = 1e-5
N_MOD = 6

kernel_name = 'hybrid_rwkv7_pool_peer_deepnorm_adaln'


def layer_norm(x, g, b):
    xf = x.astype(jnp.float32)
    mu = jnp.mean(xf, axis=-1, keepdims=True)
    var = jnp.mean(jnp.square(xf - mu), axis=-1, keepdims=True)
    return ((xf - mu) * lax.rsqrt(var + LN_EPS)).astype(x.dtype) * g + b


def wkv7_scan(r, decay, k, v, a_vec, b_vec):
    B, S, H, N = r.shape

    def step(state, inp):
        r_t, w_t, k_t, v_t, a_t, b_t = inp
        sa = jnp.einsum('bhvk,bhk->bhv', state, a_t)
        state = (state * w_t[:, :, None, :] + sa[..., None] * b_t[:, :, None, :]
                 + v_t[..., None] * k_t[:, :, None, :])
        y = jnp.einsum('bhvk,bhk->bhv', state, r_t)
        return state, y

    xs = tuple(jnp.moveaxis(t, 1, 0) for t in (r, decay, k, v, a_vec, b_vec))
    s0 = jnp.zeros((B, H, N, N), jnp.float32)
    _, ys = lax.scan(step, s0, xs)
    return jnp.moveaxis(ys, 0, 1)


def rwkv7_time_mix(h, mu, w_rkv, w0, w1, w2, a0, a1, a2, g1, g2, k_k, k_a, r_k, lnx_g, lnx_b, w_o):
    B, S, D = h.shape
    H, N = RWKV_HEADS, RWKV_HEAD_SIZE
    f32 = jnp.float32
    xx = jnp.pad(h, ((0, 0), (1, 0), (0, 0)))[:, :-1] - h
    xs = h[None] + xx[None] * mu[:, None, None, :]
    r, k, v = jnp.einsum('nbsd,nde->nbse', xs[:3], w_rkv)
    w = -jax.nn.softplus(-(w0 + jnp.tanh(xs[3] @ w1) @ w2).astype(f32)) - 0.5
    decay = jnp.exp(-jnp.exp(w))
    a = jax.nn.sigmoid(a0 + (xs[4] @ a1) @ a2)
    g = jax.nn.sigmoid(xs[5] @ g1) @ g2
    kk = (k * k_k).astype(f32).reshape(B, S, H, N)
    kk = kk * lax.rsqrt(jnp.maximum(jnp.sum(kk * kk, axis=-1, keepdims=True), 1e-24))
    k = k * (1 + (a - 1) * k_a)
    rh, kh, vh, ah = (t.astype(f32).reshape(B, S, H, N) for t in (r, k, v, a))
    y = wkv7_scan(rh, decay.reshape(B, S, H, N), kh, vh, -kk, kk * ah)
    ym = jnp.mean(y, axis=-1, keepdims=True)
    yv = jnp.mean(jnp.square(y - ym), axis=-1, keepdims=True)
    y = ((y - ym) * lax.rsqrt(yv + GN_EPS)).reshape(B, S, D) * lnx_g + lnx_b
    bonus = jnp.sum(rh * kh * r_k, axis=-1, keepdims=True) * vh
    y = (y + bonus.reshape(B, S, D)).astype(h.dtype)
    return (y * g) @ w_o


def multiscale_pool_mix(h, w_in, w_grp, scale, w_out):
    B, S, D = h.shape
    z = (h @ w_in).astype(jnp.float32).reshape(B, S, POOL_GROUPS, POOL_GROUP_WIDTH)
    cs = jnp.cumsum(z, axis=1)
    t = jnp.arange(S)
    pooled = []
    for gi, win in enumerate(POOL_WINDOWS):
        cg = cs[:, :, gi]
        lag = jnp.pad(cg, ((0, 0), (win, 0), (0, 0)))[:, :S]
        cnt = jnp.minimum(t + 1, win).astype(jnp.float32)[None, :, None]
        pooled.append((cg - lag) / cnt)
    p = (jnp.stack(pooled, axis=2) - z).astype(h.dtype)
    y = jnp.einsum('bsgc,gce->bsge', p, w_grp).reshape(B, S, D) * scale
    return y @ w_out


def peer_ffn(h, w_q, keys, u_tab, v_tab):
    B, S, D = h.shape
    K = PEER_TOPK
    q = (h @ w_q).reshape(B, S, PEER_HEADS, 2, PEER_DK_HALF)
    s = jnp.einsum('bshpd,hpkd->bshpk', q, keys).astype(jnp.float32)
    s_top, i_top = lax.top_k(s, K)
    cand = (s_top[..., 0, :, None] + s_top[..., 1, None, :]).reshape(B, S, PEER_HEADS, K * K)
    cand_idx = (i_top[..., 0, :, None] * N_KEYS + i_top[..., 1, None, :]).reshape(B, S, PEER_HEADS, K * K)
    best, pos = lax.top_k(cand, K)
    idx = jnp.take_along_axis(cand_idx, pos, axis=-1)
    gate = jax.nn.softmax(best, axis=-1).astype(h.dtype)
    nb = B * (S // PEER_BLOCK)
    hb = h.reshape(nb, PEER_BLOCK, D)
    ib = idx.reshape(nb, PEER_BLOCK, PEER_HEADS, K)
    gb = gate.reshape(nb, PEER_BLOCK, PEER_HEADS, K)

    def block(args):
        hx, ix, gx = args
        u_sel = jnp.take(u_tab, ix, axis=0)
        act = jax.nn.gelu(jnp.einsum('thkd,td->thk', u_sel, hx), approximate=False)
        v_sel = jnp.take(v_tab, ix, axis=0)
        return jnp.einsum('thk,thkd->td', gx * act, v_sel)

    return lax.map(block, (hb, ib, gb)).reshape(B, S, D)


def _normal(key, shape, scale):
    return jax.random.normal(key, shape, jnp.float32) * scale


def setup_inputs(seed: int = 0) -> dict:
    key = jax.random.key(seed)
    ks = list(jax.random.split(key, 32))
    D = D_MODEL
    n_rwkv = (DEPTH + 1) // 2
    n_pool = DEPTH // 2
    H, N = RWKV_HEADS, RWKV_HEAD_SIZE
    return {
        'x': _normal(ks[0], (BATCH, SEQ, D), 1.0),
        'c': _normal(ks[1], (BATCH, D), 1.0),
        'ada_w': _normal(ks[2], (DEPTH, D, N_MOD * D), D ** -0.5),
        'ada_b': _normal(ks[3], (DEPTH, N_MOD * D), 0.02),
        'ln_g': 1.0 + _normal(ks[4], (DEPTH, 2, D), 0.05),
        'ln_b': _normal(ks[5], (DEPTH, 2, D), 0.02),
        'rw_mu': jax.random.uniform(ks[6], (n_rwkv, N_SHIFT_BRANCHES, D), jnp.float32),
        'rw_w_rkv': _normal(ks[7], (n_rwkv, 3, D, D), D ** -0.5),
        'rw_w0': jax.random.uniform(ks[8], (n_rwkv, D), jnp.float32, -6.0, -1.0),
        'rw_w1': _normal(ks[9], (n_rwkv, D, DECAY_LORA), 0.5 * D ** -0.5),
        'rw_w2': _normal(ks[10], (n_rwkv, DECAY_LORA, D), 0.5 * DECAY_LORA ** -0.5),
        'rw_a0': _normal(ks[11], (n_rwkv, D), 0.1),
        'rw_a1': _normal(ks[12], (n_rwkv, D, AAA_LORA), D ** -0.5),
        'rw_a2': _normal(ks[13], (n_rwkv, AAA_LORA, D), AAA_LORA ** -0.5),
        'rw_g1': _normal(ks[14], (n_rwkv, D, GATE_LORA), D ** -0.5),
        'rw_g2': _normal(ks[15], (n_rwkv, GATE_LORA, D), GATE_LORA ** -0.5),
        'rw_k_k': 0.85 + _normal(ks[16], (n_rwkv, D), 0.05),
        'rw_k_a': 1.0 + _normal(ks[17], (n_rwkv, D), 0.05),
        'rw_r_k': _normal(ks[18], (n_rwkv, H, N), 0.1),
        'rw_lnx_g': 1.0 + _normal(ks[19], (n_rwkv, D), 0.05),
        'rw_lnx_b': _normal(ks[20], (n_rwkv, D), 0.02),
        'rw_w_o': _normal(ks[21], (n_rwkv, D, D), BETA * D ** -0.5),
        'pl_w_in': _normal(ks[22], (n_pool, D, D), D ** -0.5),
        'pl_w_grp': _normal(ks[23], (n_pool, POOL_GROUPS, POOL_GROUP_WIDTH, POOL_GROUP_WIDTH), POOL_GROUP_WIDTH ** -0.5),
        'pl_scale': 1.0 + _normal(ks[24], (n_pool, D), 0.1),
        'pl_w_out': _normal(ks[25], (n_pool, D, D), BETA * D ** -0.5),
        'pe_w_q': _normal(ks[26], (DEPTH, D, PEER_HEADS * PEER_DK), D ** -0.5),
        'pe_keys': _normal(ks[27], (DEPTH, PEER_HEADS, 2, N_KEYS, PEER_DK_HALF), PEER_DK_HALF ** -0.5),
        'pe_u': _normal(ks[28], (DEPTH, N_EXPERTS, D), D ** -0.5),
        'pe_v': _normal(ks[29], (DEPTH, N_EXPERTS, D), BETA),
    }


def reference(x, c, ada_w, ada_b, ln_g, ln_b, rw_mu, rw_w_rkv, rw_w0, rw_w1, rw_w2, rw_a0, rw_a1, rw_a2,
              rw_g1, rw_g2, rw_k_k, rw_k_a, rw_r_k, rw_lnx_g, rw_lnx_b, rw_w_o,
              pl_w_in, pl_w_grp, pl_scale, pl_w_out, pe_w_q, pe_keys, pe_u, pe_v):
    B = x.shape[0]
    cond = jax.nn.silu(c)
    for i in range(DEPTH):
        mod = (cond @ ada_w[i] + ada_b[i]).reshape(B, N_MOD, D_MODEL)[:, :, None, :]
        shift1, scale1, gate1, shift2, scale2, gate2 = (mod[:, n] for n in range(N_MOD))
        j = i // N_MIXERS
        h = x * (1 + scale1) + shift1
        if i % N_MIXERS == 0:
            y = rwkv7_time_mix(h, rw_mu[j], rw_w_rkv[j], rw_w0[j], rw_w1[j], rw_w2[j], rw_a0[j], rw_a1[j],
                               rw_a2[j], rw_g1[j], rw_g2[j], rw_k_k[j], rw_k_a[j], rw_r_k[j],
                               rw_lnx_g[j], rw_lnx_b[j], rw_w_o[j])
        else:
            y = multiscale_pool_mix(h, pl_w_in[j], pl_w_grp[j], pl_scale[j], pl_w_out[j])
        x = layer_norm(ALPHA * x + gate1 * y, ln_g[i, 0], ln_b[i, 0])
        h = x * (1 + scale2) + shift2
        y = peer_ffn(h, pe_w_q[i], pe_keys[i], pe_u[i], pe_v[i])
        x = layer_norm(ALPHA * x + gate2 * y, ln_g[i, 1], ln_b[i, 1])
    return x
```

```python
import functools
import math

import jax
import jax.numpy as jnp
from jax import lax
from jax.experimental import pallas as pl
from jax.experimental.pallas import tpu as pltpu

F32 = jnp.float32
BF16 = jnp.bfloat16
HI = lax.Precision.HIGHEST

HEAD = 64
PAIR = 2 * HEAD
CHUNK = 64
GN_EPS = 64e-5
LN_EPS = 1e-5
DEPTH = 2
ALPHA = (2 * DEPTH) ** 0.25
POOL_WINDOWS = (2, 4, 8, 16)
POOL_HALO = 16
N_KEYS = 128
PEER_HEADS = 8
PEER_TOPK = 16
LOG2E = 1.4426950408889634
INV_SQRT2 = 0.7071067811865476
NEG_INF = float("-inf")
V7X_VMEM_LIMIT = 56 * 1024 * 1024

NT_DIMS = (((1,), (1,)), ((), ()))
TN_DIMS = (((0,), (0,)), ((), ()))


def _dot(a, b, precision=None):
    return jnp.dot(a, b, precision=precision, preferred_element_type=F32)


def _dot_nt(a, b):
    return lax.dot_general(a, b, NT_DIMS, preferred_element_type=F32)


def _dot_tn(a, b):
    return lax.dot_general(a, b, TN_DIMS, preferred_element_type=F32)


def _params(sem, vmem=V7X_VMEM_LIMIT):
    return pltpu.CompilerParams(dimension_semantics=sem, vmem_limit_bytes=vmem)


def _layer_norm_rows(z, g, b):
    mu = jnp.mean(z, axis=-1, keepdims=True)
    d = z - mu
    var = jnp.mean(d * d, axis=-1, keepdims=True)
    return d * lax.rsqrt(var + LN_EPS) * g + b


def _mod_kernel(c_ref, w_ref, b_ref, o_ref):
    c = c_ref[...]
    cond = c * jax.nn.sigmoid(c)
    o_ref[0] = _dot(cond, w_ref[0], HI) + b_ref[0]


def _adaln_mod(c, ada_w, ada_b):
    depth, d, nd = ada_w.shape
    b = c.shape[0]
    nmod = nd // d
    out = pl.pallas_call(
        _mod_kernel,
        name="adaln_mod",
        grid=(depth, nmod),
        in_specs=[
            pl.BlockSpec((b, d), lambda l, n: (0, 0)),
            pl.BlockSpec((1, d, d), lambda l, n: (l, 0, n)),
            pl.BlockSpec((1, 1, d), lambda l, n: (l, 0, n)),
        ],
        out_specs=pl.BlockSpec((1, b, d), lambda l, n: (l, 0, n)),
        out_shape=jax.ShapeDtypeStruct((depth, b, nd), F32),
        compiler_params=_params(("arbitrary", "arbitrary")),
    )(c, ada_w, ada_b.reshape(depth, 1, nd))
    return out.reshape(depth, b, nmod, d)


def _softplus(z):
    return jnp.maximum(z, 0.0) + jnp.log1p(jnp.exp(-jnp.abs(z)))


def _rwkv_proj_kernel(x_ref, xp_ref, mod_ref, mu_ref, vec_ref, wrkv_ref, w1_ref, w2_ref, a1_ref, a2_ref,
                      g1_ref, g2_ref, r_ref, lw_ref, k_ref, v_ref, kk_ref, a_ref, g_ref):
    s = pl.program_id(1)
    m = mod_ref[0]
    shift, scale = m[0:1], m[1:2]
    h = x_ref[0] * (1.0 + scale) + shift
    prev = xp_ref[0][7:8] * (1.0 + scale) + shift
    prev = jnp.where(s == 0, 0.0, prev)
    row = lax.broadcasted_iota(jnp.int32, h.shape, 0)
    hprev = jnp.where(row == 0, prev, pltpu.roll(h, 1, 0))
    xx = hprev - h
    mu = mu_ref[...]

    def mix(n):
        return (h + xx * mu[n:n + 1]).astype(BF16)

    vec = vec_ref[...]
    w0, a0, k_k, k_a = vec[0:1], vec[1:2], vec[2:3], vec[3:4]
    r = _dot(mix(0), wrkv_ref[0])
    k = _dot(mix(1), wrkv_ref[1])
    v = _dot(mix(2), wrkv_ref[2])
    wl = w0 + _dot(jnp.tanh(_dot(mix(3), w1_ref[...])).astype(BF16), w2_ref[...])
    w = -_softplus(-wl) - 0.5
    a = jax.nn.sigmoid(a0 + _dot(_dot(mix(4), a1_ref[...]).astype(BF16), a2_ref[...]))
    g = _dot(jax.nn.sigmoid(_dot(mix(5), g1_ref[...])).astype(BF16), g2_ref[...])
    r_ref[0] = r
    lw_ref[0] = -jnp.exp(w)
    k_ref[0] = k * (1.0 + (a - 1.0) * k_a)
    v_ref[0] = v
    kk_ref[0] = k * k_k
    a_ref[0] = a
    g_ref[0] = g.astype(BF16)


def _pad_cols(w, n):
    return jnp.pad(w, ((0, 0), (0, n - w.shape[1])))


def _pad_rows(w, n):
    return jnp.pad(w, ((0, n - w.shape[0]), (0, 0)))


def _rwkv_proj(x, mod_l, mu, w_rkv, w0, w1, w2, a0, a1, a2, g1, g2, k_k, k_a, ts):
    b, s, d = x.shape
    lora = 128
    glora = 256
    vec = jnp.zeros((8, d), F32).at[0].set(w0).at[1].set(a0).at[2].set(k_k).at[3].set(k_a)
    tile = pl.BlockSpec((1, ts, d), lambda i, j: (i, j, 0))
    full2 = lambda shape: pl.BlockSpec(shape, lambda i, j: (0, 0))
    outs = pl.pallas_call(
        _rwkv_proj_kernel,
        name="rwkv_proj",
        grid=(b, s // ts),
        in_specs=[
            tile,
            pl.BlockSpec((1, 8, d), lambda i, j: (i, jnp.maximum(j * (ts // 8) - 1, 0), 0)),
            pl.BlockSpec((1, 6, d), lambda i, j: (i, 0, 0)),
            full2((6, d)),
            full2((8, d)),
            pl.BlockSpec((3, d, d), lambda i, j: (0, 0, 0)),
            full2((d, lora)), full2((lora, d)),
            full2((d, lora)), full2((lora, d)),
            full2((d, glora)), full2((glora, d)),
        ],
        out_specs=[tile] * 7,
        out_shape=[jax.ShapeDtypeStruct((b, s, d), F32)] * 6 + [jax.ShapeDtypeStruct((b, s, d), BF16)],
        compiler_params=_params(("arbitrary", "arbitrary")),
    )(x, x, mod_l, mu, vec, w_rkv.astype(BF16),
      _pad_cols(w1, lora).astype(BF16), _pad_rows(w2, lora).astype(BF16),
      _pad_cols(a1, lora).astype(BF16), _pad_rows(a2, lora).astype(BF16),
      _pad_cols(g1, glora).astype(BF16), _pad_rows(g2, glora).astype(BF16))
    return outs


def _head_masks(shape):
    lane = lax.broadcasted_iota(jnp.int32, shape, len(shape) - 1)
    first = (lane % PAIR) < HEAD
    return first, jnp.logical_not(first)


def _stack_heads(z):
    m0, m1 = _head_masks(z.shape)
    return jnp.concatenate([jnp.where(m0, z, 0.0), jnp.where(m1, z, 0.0)], axis=0)


def _wkv_kernel(r_ref, lw_ref, k_ref, v_ref, kk_ref, a_ref, rk_ref, lng_ref, lnb_ref, y_ref,
                s_ref, rh_ref, y0_ref, m_ref, c_ref, yb_ref, *, nchunk):
    L = CHUNK

    @pl.when(pl.program_id(2) == 0)
    def _():
        s_ref[...] = jnp.zeros_like(s_ref)

    r2 = lax.broadcasted_iota(jnp.int32, (PAIR, PAIR), 0)
    c2 = lax.broadcasted_iota(jnp.int32, (PAIR, PAIR), 1)
    same_head = (r2 // HEAD) == (c2 // HEAD)
    eye = r2 == c2
    bd_ones = jnp.where(same_head, 1.0, 0.0).astype(F32)
    tr = lax.broadcasted_iota(jnp.int32, (L, PAIR), 0)
    tc = lax.broadcasted_iota(jnp.int32, (L, PAIR), 1) % HEAD
    strict = tr > tc
    incl = tr >= tc

    def off_diag(size):
        same = (tr // (2 * size)) == (tc // (2 * size))
        return same & ((tr % (2 * size)) >= size) & ((tc % (2 * size)) < size)
    lr = lax.broadcasted_iota(jnp.int32, (L, L), 0)
    lc = lax.broadcasted_iota(jnp.int32, (L, L), 1)
    ltri = jnp.where(lr >= lc, 1.0, 0.0).astype(F32)

    def chunk_terms(c, carry):
        sl = pl.ds(pl.multiple_of(c * L, L), L)
        r = r_ref[0, sl, :]
        lw = lw_ref[0, sl, :]
        k = k_ref[0, sl, :]
        v = v_ref[0, sl, :]
        kkr = kk_ref[0, sl, :]
        a = a_ref[0, sl, :]
        n2 = _dot(kkr * kkr, bd_ones, HI)
        kk = kkr * lax.rsqrt(jnp.maximum(n2, 1e-24))
        av = -kk
        bv = kk * a
        cs = _dot(ltri, lw, HI)
        cs_last = cs[L - 1:L, :]
        rt = r * jnp.exp(cs)
        at = av * jnp.exp(cs - lw)
        pinv = jnp.exp(-cs)
        bt = bv * pinv
        kt = k * pinv
        prem = jnp.exp(cs_last - cs)
        bc = bv * prem
        kc = k * prem
        lhs = jnp.concatenate([at, rt], axis=0).astype(BF16)
        rhs = jnp.concatenate([_stack_heads(bt), _stack_heads(kt)], axis=0).astype(BF16)
        o = _dot_nt(lhs, rhs)
        nmat = jnp.where(strict, o[:L, :PAIR], 0.0)
        akm = jnp.where(strict, o[:L, PAIR:], 0.0)
        rbm = jnp.where(incl, o[L:, :PAIR], 0.0)
        rkm = jnp.where(incl, o[L:, PAIR:], 0.0)
        vst = _stack_heads(v).astype(BF16)
        tinv = jnp.where(tr == tc, 1.0, 0.0) + jnp.where(off_diag(1), nmat, 0.0)
        size = 2
        while size < L:
            noff = jnp.where(off_diag(size), nmat, 0.0)
            tn = _dot(tinv.astype(BF16), _stack_heads(noff).astype(BF16))
            tinv = tinv + _dot(tn.astype(BF16), _stack_heads(tinv).astype(BF16))
            size *= 2
        z = jnp.concatenate([at, _dot(akm.astype(BF16), vst)], axis=1)
        z = _dot(tinv.astype(BF16), _stack_heads(z).astype(BF16))
        tmp = _dot(rbm.astype(BF16), _stack_heads(z).astype(BF16))
        rh_ref[c] = rt + tmp[:, :PAIR]
        y0_ref[c] = tmp[:, PAIR:] + _dot(rkm.astype(BF16), vst)
        bz = _dot_tn(bc.astype(BF16), z.astype(BF16))
        kv = _dot_tn(kc.astype(BF16), v.astype(BF16))
        pl_b = jnp.broadcast_to(jnp.exp(cs_last), (PAIR, PAIR))
        m_ref[c] = jnp.where(eye, pl_b, 0.0) + jnp.where(same_head, bz[:, :PAIR], 0.0)
        c_ref[c] = jnp.where(same_head, bz[:, PAIR:] + kv, 0.0)
        return carry

    lax.fori_loop(0, nchunk, chunk_terms, 0)

    def chunk_state(c, carry):
        sb = s_ref[...].astype(BF16)
        yb_ref[pl.ds(pl.multiple_of(c * L, L), L), :] = _dot(rh_ref[c].astype(BF16), sb) + y0_ref[c]
        s_ref[...] = _dot(m_ref[c].astype(BF16), sb) + c_ref[c]
        return carry

    lax.fori_loop(0, nchunk, chunk_state, 0)

    y = yb_ref[...]
    bd_avg = bd_ones * (1.0 / HEAD)
    ym = _dot(y, bd_avg, HI)
    d = y - ym
    yv = _dot(d * d, bd_avg, HI)
    yn = d * lax.rsqrt(yv + GN_EPS) * lng_ref[...] + lnb_ref[...]
    rr = r_ref[0]
    bonus = _dot(rr * k_ref[0] * rk_ref[...], bd_ones, HI) * v_ref[0]
    y_ref[0] = yn + bonus


def _wkv_scan(r, lw, k, v, kkr, a, r_k, lnx_g, lnx_b, tc):
    b, s, d = r.shape
    nchunk = tc // CHUNK
    tile = pl.BlockSpec((1, tc, PAIR), lambda i, p, j: (i, j, p))
    row = pl.BlockSpec((1, PAIR), lambda i, p, j: (0, p))
    return pl.pallas_call(
        functools.partial(_wkv_kernel, nchunk=nchunk),
        name="wkv_scan",
        grid=(b, d // PAIR, s // tc),
        in_specs=[tile] * 6 + [row] * 3,
        out_specs=tile,
        out_shape=jax.ShapeDtypeStruct((b, s, d), F32),
        scratch_shapes=[
            pltpu.VMEM((PAIR, PAIR), F32),
            pltpu.VMEM((nchunk, CHUNK, PAIR), F32),
            pltpu.VMEM((nchunk, CHUNK, PAIR), F32),
            pltpu.VMEM((nchunk, PAIR, PAIR), F32),
            pltpu.VMEM((nchunk, PAIR, PAIR), F32),
            pltpu.VMEM((tc, PAIR), F32),
        ],
        compiler_params=_params(("arbitrary", "arbitrary", "arbitrary")),
    )(r, lw, k, v, kkr, a, r_k.reshape(1, d), lnx_g.reshape(1, d), lnx_b.reshape(1, d))


def _mix_out_kernel(y_ref, g_ref, x_ref, mod_ref, wo_ref, ln_ref, o_ref):
    m = mod_ref[0]
    gate = m[2:3]
    yg = (y_ref[0] * g_ref[0].astype(F32)).astype(BF16)
    o = _dot(yg, wo_ref[...])
    z = ALPHA * x_ref[0] + gate * o
    ln = ln_ref[...]
    o_ref[0] = _layer_norm_rows(z, ln[0:1], ln[1:2])


def _mix_out(y, g, x, mod_l, w_o, ln_g, ln_b, ts):
    b, s, d = x.shape
    tile = pl.BlockSpec((1, ts, d), lambda i, j: (i, j, 0))
    ln = jnp.zeros((8, d), F32).at[0].set(ln_g).at[1].set(ln_b)
    return pl.pallas_call(
        _mix_out_kernel,
        name="mix_out",
        grid=(b, s // ts),
        in_specs=[tile, tile, tile,
                  pl.BlockSpec((1, 6, d), lambda i, j: (i, 0, 0)),
                  pl.BlockSpec((d, d), lambda i, j: (0, 0)),
                  pl.BlockSpec((8, d), lambda i, j: (0, 0))],
        out_specs=tile,
        out_shape=jax.ShapeDtypeStruct((b, s, d), F32),
        compiler_params=_params(("arbitrary", "arbitrary")),
    )(y, g, x, mod_l, w_o.astype(BF16), ln)


def _pool_kernel(x_ref, mod_ref, win_ref, wgrp_ref, sc_ref, wout_ref, ln_ref, o_ref, zext_ref, *, ts):
    s = pl.program_id(1)
    m = mod_ref[0]
    shift, scale, gate = m[0:1], m[1:2], m[2:3]
    x = x_ref[0]
    h = x * (1.0 + scale) + shift
    z = _dot(h.astype(BF16), win_ref[...])

    @pl.when(s == 0)
    def _():
        zext_ref[0:POOL_HALO, :] = jnp.zeros((POOL_HALO, z.shape[1]), F32)

    zext_ref[POOL_HALO:POOL_HALO + ts, :] = z
    pos = s * ts + lax.broadcasted_iota(jnp.int32, (ts, 1), 0)
    gw = z.shape[1] // len(POOL_WINDOWS)
    parts = []
    for gi, win in enumerate(POOL_WINDOWS):
        lo = gi * gw
        zg = z[:, lo:lo + gw]
        acc = zg
        for back in range(1, win):
            acc = acc + zext_ref[POOL_HALO - back:POOL_HALO - back + ts, lo:lo + gw]
        cnt = jnp.minimum(pos + 1, win).astype(F32)
        p = acc / cnt - zg
        parts.append(_dot(p.astype(BF16), wgrp_ref[gi]))
    y = jnp.concatenate(parts, axis=1) * sc_ref[...]
    o = _dot(y.astype(BF16), wout_ref[...])
    zext_ref[0:POOL_HALO, :] = zext_ref[ts:ts + POOL_HALO, :]
    ln = ln_ref[...]
    o_ref[0] = _layer_norm_rows(ALPHA * x + gate * o, ln[0:1], ln[1:2])


def _pool_mix(x, mod_l, w_in, w_grp, scale, w_out, ln_g, ln_b, ts):
    b, s, d = x.shape
    ng, gw, _ = w_grp.shape
    tile = pl.BlockSpec((1, ts, d), lambda i, j: (i, j, 0))
    ln = jnp.zeros((8, d), F32).at[0].set(ln_g).at[1].set(ln_b)
    return pl.pallas_call(
        functools.partial(_pool_kernel, ts=ts),
        name="pool_mix",
        grid=(b, s // ts),
        in_specs=[tile,
                  pl.BlockSpec((1, 6, d), lambda i, j: (i, 0, 0)),
                  pl.BlockSpec((d, d), lambda i, j: (0, 0)),
                  pl.BlockSpec((ng, gw, gw), lambda i, j: (0, 0, 0)),
                  pl.BlockSpec((1, d), lambda i, j: (0, 0)),
                  pl.BlockSpec((d, d), lambda i, j: (0, 0)),
                  pl.BlockSpec((8, d), lambda i, j: (0, 0))],
        out_specs=tile,
        out_shape=jax.ShapeDtypeStruct((b, s, d), F32),
        scratch_shapes=[pltpu.VMEM((ts + POOL_HALO, d), F32)],
        compiler_params=_params(("arbitrary", "arbitrary")),
    )(x, mod_l, w_in.astype(BF16), w_grp.astype(BF16), scale.reshape(1, d), w_out.astype(BF16), ln)


def _fold_kernel(keys_ref, wq_ref, o_ref):
    o_ref[0] = lax.dot_general(keys_ref[0, 0], wq_ref[0], NT_DIMS, precision=HI,
                               preferred_element_type=F32).astype(BF16)


def _peer_fold(keys, w_q):
    depth, nh, two, nk, dk = keys.shape
    d = w_q.shape[1]
    nb = nh * two
    return pl.pallas_call(
        _fold_kernel,
        name="peer_fold",
        grid=(depth, nb),
        in_specs=[pl.BlockSpec((1, 1, nk, dk), lambda l, j: (l, j, 0, 0)),
                  pl.BlockSpec((1, d, dk), lambda l, j: (l, 0, j))],
        out_specs=pl.BlockSpec((1, nk, d), lambda l, j: (l, j, 0)),
        out_shape=jax.ShapeDtypeStruct((depth, nb * nk, d), BF16),
        compiler_params=_params(("arbitrary", "arbitrary")),
    )(keys.reshape(depth, nb, nk, dk), w_q)


N_TOP = PEER_TOPK + 1
CAND = [(a, b) for a in range(N_TOP) for b in range(N_TOP) if (a + 1) * (b + 1) <= N_TOP]
CAND_ROWS = -(-len(CAND) // 8) * 8
TOP_ROWS = -(-N_TOP // 8) * 8


def _top_values(cur, out_ref, n):
    row = lax.broadcasted_iota(jnp.int32, cur.shape, 0)

    def body(i, cur):
        mx = jnp.max(cur, axis=0, keepdims=True)
        out_ref[pl.ds(i, 1), :] = mx
        first = jnp.min(jnp.where(cur == mx, row, cur.shape[0]), axis=0, keepdims=True)
        return jnp.where(row == first, NEG_INF, cur)

    lax.fori_loop(0, n, body, cur)


def _route_kernel(x_ref, mod_ref, wf_ref, h2_ref, s2p_ref, del_ref, th_ref, ta_ref, tb_ref, tc_ref):
    hd = pl.program_id(1)
    m = mod_ref[0]
    shift, scale = m[3:4], m[4:5]

    @pl.when(hd == 0)
    def _():
        h2_ref[...] = (x_ref[...] * (1.0 + scale) + shift).astype(BF16)

    h2 = h2_ref[...]
    s1 = _dot_nt(wf_ref[0:N_KEYS, :], h2)
    s2 = _dot_nt(wf_ref[N_KEYS:2 * N_KEYS, :], h2)
    _top_values(s1, ta_ref, N_TOP)
    _top_values(s2, tb_ref, N_TOP)
    ta = ta_ref[...]
    tb = tb_ref[...]
    rows = [ta[a:a + 1] + tb[b:b + 1] for a, b in CAND]
    rows += [jnp.full_like(rows[0], NEG_INF)] * (CAND_ROWS - len(CAND))
    cand = jnp.concatenate(rows, axis=0)
    _top_values(cand, tc_ref, N_TOP)
    tcv = tc_ref[...]
    tau = 0.5 * (tcv[PEER_TOPK - 1:PEER_TOPK] + tcv[PEER_TOPK:PEER_TOPK + 1])
    top = ta[0:1] + tb[0:1]
    zsum = jnp.sum(jnp.where(cand >= tau, jnp.exp(cand - top), 0.0), axis=0, keepdims=True)
    logz = jnp.log(zsum)
    s2p_ref[0] = (s2 - tb[0:1] - logz) * LOG2E
    del_ref[0] = (s1 - ta[0:1]) * LOG2E
    th_ref[0] = (tau - top - logz) * LOG2E


def _peer_route(x2d, mod_l, wf, tt, tiles_per_batch):
    t, d = x2d.shape
    nh = PEER_HEADS
    return pl.pallas_call(
        _route_kernel,
        name="peer_route",
        grid=(t // tt, nh),
        in_specs=[pl.BlockSpec((tt, d), lambda i, h: (i, 0)),
                  pl.BlockSpec((1, 6, d), lambda i, h: (i // tiles_per_batch, 0, 0)),
                  pl.BlockSpec((2 * N_KEYS, d), lambda i, h: (h, 0))],
        out_specs=[pl.BlockSpec((tt, d), lambda i, h: (i, 0)),
                   pl.BlockSpec((1, N_KEYS, tt), lambda i, h: (h, 0, i)),
                   pl.BlockSpec((1, N_KEYS, tt), lambda i, h: (h, 0, i)),
                   pl.BlockSpec((1, 1, tt), lambda i, h: (h, 0, i))],
        out_shape=[jax.ShapeDtypeStruct((t, d), BF16),
                   jax.ShapeDtypeStruct((nh, N_KEYS, t), F32),
                   jax.ShapeDtypeStruct((nh, N_KEYS, t), F32),
                   jax.ShapeDtypeStruct((nh, 1, t), F32)],
        scratch_shapes=[pltpu.VMEM((TOP_ROWS, tt), F32),
                        pltpu.VMEM((TOP_ROWS, tt), F32),
                        pltpu.VMEM((TOP_ROWS, tt), F32)],
        compiler_params=_params(("arbitrary", "arbitrary")),
    )(x2d, mod_l, wf)


ROWS = 16


def _dense_kernel(h2_ref, s2p_ref, del_ref, th_ref, u_ref, vt_ref, x_ref, mod_ref, ln_ref, o_ref,
                  acc_ref, z_ref, p_ref, *, eb):
    e = pl.program_id(1)

    @pl.when(e == 0)
    def _():
        acc_ref[...] = jnp.zeros_like(acc_ref)

    z_ref[...] = _dot_nt(u_ref[...], h2_ref[...])
    nsub = N_KEYS // ROWS
    for il in range(eb // N_KEYS):
        i1 = e * (eb // N_KEYS) + il

        def body(jb, carry):
            j0 = pl.multiple_of(jb * ROWS, ROWS)
            zz = z_ref[pl.ds(il * N_KEYS + j0, ROWS), :]
            act = 0.5 * zz * (1.0 + lax.erf(zz * INV_SQRT2))
            gsum = jnp.zeros_like(zz)
            for hd in range(PEER_HEADS):
                xl = s2p_ref[hd, pl.ds(j0, ROWS), :] + del_ref[hd, pl.ds(i1, 1), :]
                gsum = gsum + jnp.where(xl >= th_ref[hd], jnp.exp2(xl), 0.0)
            p_ref[pl.ds(il * N_KEYS + j0, ROWS), :] = (act * gsum).astype(BF16)
            return carry

        lax.fori_loop(0, nsub, body, 0)
    acc_ref[...] += _dot(vt_ref[...], p_ref[...])

    @pl.when(e == pl.num_programs(1) - 1)
    def _():
        m = mod_ref[0]
        gate = m[5:6]
        ln = ln_ref[...]
        y = acc_ref[...].T
        o_ref[...] = _layer_norm_rows(ALPHA * x_ref[...] + gate * y, ln[0:1], ln[1:2])


def _peer_dense(h2, s2p, dlt, th, u_bf, vt_bf, x2d, mod_l, ln_g, ln_b, tt, eb, tiles_per_batch):
    t, d = x2d.shape
    ne = u_bf.shape[0]
    nh = PEER_HEADS
    ln = jnp.zeros((8, d), F32).at[0].set(ln_g).at[1].set(ln_b)
    return pl.pallas_call(
        functools.partial(_dense_kernel, eb=eb),
        name="peer_dense",
        grid=(t // tt, ne // eb),
        in_specs=[pl.BlockSpec((tt, d), lambda i, e: (i, 0)),
                  pl.BlockSpec((nh, N_KEYS, tt), lambda i, e: (0, 0, i)),
                  pl.BlockSpec((nh, N_KEYS, tt), lambda i, e: (0, 0, i)),
                  pl.BlockSpec((nh, 1, tt), lambda i, e: (0, 0, i)),
                  pl.BlockSpec((eb, d), lambda i, e: (e, 0)),
                  pl.BlockSpec((d, eb), lambda i, e: (0, e)),
                  pl.BlockSpec((tt, d), lambda i, e: (i, 0)),
                  pl.BlockSpec((1, 6, d), lambda i, e: (i // tiles_per_batch, 0, 0)),
                  pl.BlockSpec((8, d), lambda i, e: (0, 0))],
        out_specs=pl.BlockSpec((tt, d), lambda i, e: (i, 0)),
        out_shape=jax.ShapeDtypeStruct((t, d), F32),
        scratch_shapes=[pltpu.VMEM((d, tt), F32),
                        pltpu.VMEM((eb, tt), F32),
                        pltpu.VMEM((eb, tt), BF16)],
        compiler_params=_params(("arbitrary", "arbitrary")),
    )(h2, s2p, dlt, th, u_bf, vt_bf, x2d, mod_l, ln)


def _peer_ffn(x, mod_l, wf, u_tab, v_tab, ln_g, ln_b, tt, eb):
    b, s, d = x.shape
    x2d = x.reshape(b * s, d)
    tpb = s // tt
    h2, s2p, dlt, th = _peer_route(x2d, mod_l, wf, tt, tpb)
    out = _peer_dense(h2, s2p, dlt, th, u_tab.astype(BF16), v_tab.astype(BF16).T, x2d, mod_l,
                      ln_g, ln_b, tt, eb, tpb)
    return out.reshape(b, s, d)


def kernel(x, c, ada_w, ada_b, ln_g, ln_b, rw_mu, rw_w_rkv, rw_w0, rw_w1, rw_w2, rw_a0, rw_a1, rw_a2, rw_g1, rw_g2, rw_k_k, rw_k_a, rw_r_k, rw_lnx_g, rw_lnx_b, rw_w_o, pl_w_in, pl_w_grp, pl_scale, pl_w_out, pe_w_q, pe_keys, pe_u, pe_v):
    b, s, d = x.shape
    depth = ada_w.shape[0]
    ts = min(256, s)
    tc = min(512, s)
    tt = min(512, s)
    eb = 512
    mod = _adaln_mod(c, ada_w, ada_b)
    wf = _peer_fold(pe_keys, pe_w_q)
    for i in range(depth):
        j = i // 2
        if i % 2 == 0:
            r, lw, k, v, kkr, a, g = _rwkv_proj(x, mod[i], rw_mu[j], rw_w_rkv[j], rw_w0[j], rw_w1[j], rw_w2[j],
                                                rw_a0[j], rw_a1[j], rw_a2[j], rw_g1[j], rw_g2[j],
                                                rw_k_k[j], rw_k_a[j], ts)
            y = _wkv_scan(r, lw, k, v, kkr, a, rw_r_k[j], rw_lnx_g[j], rw_lnx_b[j], tc)
            x = _mix_out(y, g, x, mod[i], rw_w_o[j], ln_g[i, 0], ln_b[i, 0], ts)
        else:
            x = _pool_mix(x, mod[i], pl_w_in[j], pl_w_grp[j], pl_scale[j], pl_w_out[j],
                          ln_g[i, 0], ln_b[i, 0], ts)
        x = _peer_ffn(x, mod[i], wf[i], pe_u[i], pe_v[i], ln_g[i, 1], ln_b[i, 1], tt, eb)
    return x
```

```python
import functools
import math

import jax
import jax.numpy as jnp
from jax import lax
from jax.experimental import pallas as pl
from jax.experimental.pallas import tpu as pltpu

F32 = jnp.float32
BF16 = jnp.bfloat16
HI = lax.Precision.HIGHEST

HEAD = 64
PAIR = 2 * HEAD
SUB = 8
CHUNK = 64
GN_EPS = 64e-5
LN_EPS = 1e-5
DEPTH = 2
ALPHA = (2 * DEPTH) ** 0.25
POOL_WINDOWS = (2, 4, 8, 16)
POOL_HALO = 16
N_KEYS = 128
PEER_HEADS = 8
PEER_TOPK = 16
LOG2E = 1.4426950408889634
INV_SQRT2 = 0.7071067811865476
NEG_INF = float("-inf")
V7X_VMEM_LIMIT = 56 * 1024 * 1024

NT_DIMS = (((1,), (1,)), ((), ()))
TN_DIMS = (((0,), (0,)), ((), ()))


def _dot(a, b, precision=None):
    return jnp.dot(a, b, precision=precision, preferred_element_type=F32)


def _dot_nt(a, b):
    return lax.dot_general(a, b, NT_DIMS, preferred_element_type=F32)


def _dot_tn(a, b):
    return lax.dot_general(a, b, TN_DIMS, preferred_element_type=F32)


def _params(sem, vmem=V7X_VMEM_LIMIT):
    return pltpu.CompilerParams(dimension_semantics=sem, vmem_limit_bytes=vmem)


def _layer_norm_rows(z, g, b):
    mu = jnp.mean(z, axis=-1, keepdims=True)
    d = z - mu
    var = jnp.mean(d * d, axis=-1, keepdims=True)
    return d * lax.rsqrt(var + LN_EPS) * g + b


def _mod_kernel(c_ref, w_ref, b_ref, o_ref):
    c = c_ref[...]
    cond = c * jax.nn.sigmoid(c)
    o_ref[0] = _dot(cond, w_ref[0], HI) + b_ref[0]


def _adaln_mod(c, ada_w, ada_b):
    depth, d, nd = ada_w.shape
    b = c.shape[0]
    nmod = nd // d
    out = pl.pallas_call(
        _mod_kernel,
        name="adaln_mod",
        grid=(depth, nmod),
        in_specs=[
            pl.BlockSpec((b, d), lambda l, n: (0, 0)),
            pl.BlockSpec((1, d, d), lambda l, n: (l, 0, n)),
            pl.BlockSpec((1, 1, d), lambda l, n: (l, 0, n)),
        ],
        out_specs=pl.BlockSpec((1, b, d), lambda l, n: (l, 0, n)),
        out_shape=jax.ShapeDtypeStruct((depth, b, nd), F32),
        compiler_params=_params(("arbitrary", "arbitrary")),
    )(c, ada_w, ada_b.reshape(depth, 1, nd))
    return out.reshape(depth, b, nmod, d)


def _softplus(z):
    return jnp.maximum(z, 0.0) + jnp.log1p(jnp.exp(-jnp.abs(z)))


def _rwkv_proj_kernel(x_ref, xp_ref, mod_ref, mu_ref, vec_ref, wrkv_ref, w1_ref, w2_ref, a1_ref, a2_ref,
                      g1_ref, g2_ref, r_ref, lw_ref, k_ref, v_ref, kk_ref, a_ref, g_ref):
    s = pl.program_id(1)
    m = mod_ref[0]
    shift, scale = m[0:1], m[1:2]
    h = x_ref[0] * (1.0 + scale) + shift
    prev = xp_ref[0][7:8] * (1.0 + scale) + shift
    prev = jnp.where(s == 0, 0.0, prev)
    row = lax.broadcasted_iota(jnp.int32, h.shape, 0)
    hprev = jnp.where(row == 0, prev, pltpu.roll(h, 1, 0))
    xx = hprev - h
    mu = mu_ref[...]

    def mix(n):
        return (h + xx * mu[n:n + 1]).astype(BF16)

    vec = vec_ref[...]
    w0, a0, k_k, k_a = vec[0:1], vec[1:2], vec[2:3], vec[3:4]
    r = _dot(mix(0), wrkv_ref[0])
    k = _dot(mix(1), wrkv_ref[1])
    v = _dot(mix(2), wrkv_ref[2])
    wl = w0 + _dot(jnp.tanh(_dot(mix(3), w1_ref[...])).astype(BF16), w2_ref[...])
    w = -_softplus(-wl) - 0.5
    a = jax.nn.sigmoid(a0 + _dot(_dot(mix(4), a1_ref[...]).astype(BF16), a2_ref[...]))
    g = _dot(jax.nn.sigmoid(_dot(mix(5), g1_ref[...])).astype(BF16), g2_ref[...])
    r_ref[0] = r
    lw_ref[0] = -jnp.exp(w)
    k_ref[0] = k * (1.0 + (a - 1.0) * k_a)
    v_ref[0] = v
    kk_ref[0] = k * k_k
    a_ref[0] = a
    g_ref[0] = g.astype(BF16)


def _pad_cols(w, n):
    return jnp.pad(w, ((0, 0), (0, n - w.shape[1])))


def _pad_rows(w, n):
    return jnp.pad(w, ((0, n - w.shape[0]), (0, 0)))


def _rwkv_proj(x, mod_l, mu, w_rkv, w0, w1, w2, a0, a1, a2, g1, g2, k_k, k_a, ts):
    b, s, d = x.shape
    lora = 128
    glora = 256
    vec = jnp.zeros((8, d), F32).at[0].set(w0).at[1].set(a0).at[2].set(k_k).at[3].set(k_a)
    tile = pl.BlockSpec((1, ts, d), lambda i, j: (i, j, 0))
    full2 = lambda shape: pl.BlockSpec(shape, lambda i, j: (0, 0))
    outs = pl.pallas_call(
        _rwkv_proj_kernel,
        name="rwkv_proj",
        grid=(b, s // ts),
        in_specs=[
            tile,
            pl.BlockSpec((1, 8, d), lambda i, j: (i, jnp.maximum(j * (ts // 8) - 1, 0), 0)),
            pl.BlockSpec((1, 6, d), lambda i, j: (i, 0, 0)),
            full2((6, d)),
            full2((8, d)),
            pl.BlockSpec((3, d, d), lambda i, j: (0, 0, 0)),
            full2((d, lora)), full2((lora, d)),
            full2((d, lora)), full2((lora, d)),
            full2((d, glora)), full2((glora, d)),
        ],
        out_specs=[tile] * 7,
        out_shape=[jax.ShapeDtypeStruct((b, s, d), F32)] * 6 + [jax.ShapeDtypeStruct((b, s, d), BF16)],
        compiler_params=_params(("arbitrary", "arbitrary")),
    )(x, x, mod_l, mu, vec, w_rkv.astype(BF16),
      _pad_cols(w1, lora).astype(BF16), _pad_rows(w2, lora).astype(BF16),
      _pad_cols(a1, lora).astype(BF16), _pad_rows(a2, lora).astype(BF16),
      _pad_cols(g1, glora).astype(BF16), _pad_rows(g2, glora).astype(BF16))
    return outs


def _head_masks(shape):
    lane = lax.broadcasted_iota(jnp.int32, shape, len(shape) - 1)
    first = (lane % PAIR) < HEAD
    return first, jnp.logical_not(first)


def _stack_heads(z):
    m0, m1 = _head_masks(z.shape)
    return jnp.concatenate([jnp.where(m0, z, 0.0), jnp.where(m1, z, 0.0)], axis=0)


def _wkv_kernel(r_ref, lw_ref, k_ref, v_ref, kk_ref, a_ref, rk_ref, lng_ref, lnb_ref, y_ref,
                s_ref, yb_ref, *, nchunk):
    L = CHUNK

    @pl.when(pl.program_id(2) == 0)
    def _():
        s_ref[...] = jnp.zeros_like(s_ref)

    r2 = lax.broadcasted_iota(jnp.int32, (PAIR, PAIR), 0)
    c2 = lax.broadcasted_iota(jnp.int32, (PAIR, PAIR), 1)
    same_head = (r2 // HEAD) == (c2 // HEAD)
    eye = r2 == c2
    bd_ones = jnp.where(same_head, 1.0, 0.0).astype(F32)
    tr = lax.broadcasted_iota(jnp.int32, (L, PAIR), 0)
    tc = lax.broadcasted_iota(jnp.int32, (L, PAIR), 1) % HEAD
    strict = tr > tc
    incl = tr >= tc

    def off_diag(size):
        same = (tr // (2 * size)) == (tc // (2 * size))
        return same & ((tr % (2 * size)) >= size) & ((tc % (2 * size)) < size)
    lr = lax.broadcasted_iota(jnp.int32, (L, L), 0)
    lc = lax.broadcasted_iota(jnp.int32, (L, L), 1)
    ltri = jnp.where(lr >= lc, 1.0, 0.0).astype(F32)

    chunks = range(nchunk)
    r_all = r_ref[0]
    lw_all = lw_ref[0]
    k_all = k_ref[0]
    v_all = v_ref[0]
    a_all = a_ref[0]
    kkr = kk_ref[0]
    kk = kkr * lax.rsqrt(jnp.maximum(_dot(kkr * kkr, bd_ones, HI), 1e-24))
    bv_all = kk * a_all

    def rows(x, c):
        return x[c * L:(c + 1) * L]

    cs = [_dot(ltri, rows(lw_all, c), HI) for c in chunks]
    rt, at, vst, lhs, rhs, bc, kc, p_last = [], [], [], [], [], [], [], []
    for c in chunks:
        cs_last = cs[c][L - 1:L, :]
        pinv = jnp.exp(-cs[c])
        prem = jnp.exp(cs_last - cs[c])
        rt.append(rows(r_all, c) * jnp.exp(cs[c]))
        at.append(-rows(kk, c) * jnp.exp(cs[c] - rows(lw_all, c)))
        bt = rows(bv_all, c) * pinv
        kt = rows(k_all, c) * pinv
        bc.append((rows(bv_all, c) * prem).astype(BF16))
        kc.append((rows(k_all, c) * prem).astype(BF16))
        p_last.append(jnp.exp(cs_last))
        vst.append(_stack_heads(rows(v_all, c)).astype(BF16))
        lhs.append(jnp.concatenate([at[c], rt[c]], axis=0).astype(BF16))
        rhs.append(jnp.concatenate([_stack_heads(bt), _stack_heads(kt)], axis=0).astype(BF16))
    o = [_dot_nt(lhs[c], rhs[c]) for c in chunks]
    nmat = [jnp.where(strict, o[c][:L, :PAIR], 0.0) for c in chunks]
    akv = [_dot(jnp.where(strict, o[c][:L, PAIR:], 0.0).astype(BF16), vst[c]) for c in chunks]
    ident = jnp.where(tr == tc, 1.0, 0.0)
    tinv = [ident + jnp.where(off_diag(1), nmat[c], 0.0) for c in chunks]
    size = 2
    while size < L:
        mask = off_diag(size)
        tn = [_dot(tinv[c].astype(BF16), _stack_heads(jnp.where(mask, nmat[c], 0.0)).astype(BF16)) for c in chunks]
        tinv = [tinv[c] + _dot(tn[c].astype(BF16), _stack_heads(tinv[c]).astype(BF16)) for c in chunks]
        size *= 2
    z = [_dot(tinv[c].astype(BF16), _stack_heads(jnp.concatenate([at[c], akv[c]], axis=1)).astype(BF16))
         for c in chunks]
    tmp = [_dot(jnp.where(incl, o[c][L:, :PAIR], 0.0).astype(BF16), _stack_heads(z[c]).astype(BF16))
           for c in chunks]
    rkv = [_dot(jnp.where(incl, o[c][L:, PAIR:], 0.0).astype(BF16), vst[c]) for c in chunks]
    bz = [_dot_tn(bc[c], z[c].astype(BF16)) for c in chunks]
    kv = [_dot_tn(kc[c], rows(v_all, c).astype(BF16)) for c in chunks]
    state = s_ref[...]
    for c in chunks:
        sb = state.astype(BF16)
        rh = (rt[c] + tmp[c][:, :PAIR]).astype(BF16)
        yb_ref[c * L:(c + 1) * L, :] = _dot(rh, sb) + tmp[c][:, PAIR:] + rkv[c]
        m_c = jnp.where(eye, jnp.broadcast_to(p_last[c], (PAIR, PAIR)), 0.0) + jnp.where(same_head, bz[c][:, :PAIR], 0.0)
        state = _dot(m_c.astype(BF16), sb) + jnp.where(same_head, bz[c][:, PAIR:] + kv[c], 0.0)
    s_ref[...] = state

    y = yb_ref[...]
    bd_avg = bd_ones * (1.0 / HEAD)
    ym = _dot(y, bd_avg, HI)
    d = y - ym
    yv = _dot(d * d, bd_avg, HI)
    yn = d * lax.rsqrt(yv + GN_EPS) * lng_ref[...] + lnb_ref[...]
    bonus = _dot(r_all * k_all * rk_ref[...], bd_ones, HI) * v_all
    y_ref[0] = yn + bonus


def _wkv_scan(r, lw, k, v, kkr, a, r_k, lnx_g, lnx_b, tc):
    b, s, d = r.shape
    nchunk = tc // CHUNK
    tile = pl.BlockSpec((1, tc, PAIR), lambda i, p, j: (i, j, p))
    row = pl.BlockSpec((1, PAIR), lambda i, p, j: (0, p))
    return pl.pallas_call(
        functools.partial(_wkv_kernel, nchunk=nchunk),
        name="wkv_scan",
        grid=(b, d // PAIR, s // tc),
        in_specs=[tile] * 6 + [row] * 3,
        out_specs=tile,
        out_shape=jax.ShapeDtypeStruct((b, s, d), F32),
        scratch_shapes=[
            pltpu.VMEM((PAIR, PAIR), F32),
            pltpu.VMEM((tc, PAIR), F32),
        ],
        compiler_params=_params(("arbitrary", "arbitrary", "arbitrary")),
    )(r, lw, k, v, kkr, a, r_k.reshape(1, d), lnx_g.reshape(1, d), lnx_b.reshape(1, d))


def _mix_out_kernel(y_ref, g_ref, x_ref, mod_ref, wo_ref, ln_ref, o_ref):
    m = mod_ref[0]
    gate = m[2:3]
    yg = (y_ref[0] * g_ref[0].astype(F32)).astype(BF16)
    o = _dot(yg, wo_ref[...])
    z = ALPHA * x_ref[0] + gate * o
    ln = ln_ref[...]
    o_ref[0] = _layer_norm_rows(z, ln[0:1], ln[1:2])


def _mix_out(y, g, x, mod_l, w_o, ln_g, ln_b, ts):
    b, s, d = x.shape
    tile = pl.BlockSpec((1, ts, d), lambda i, j: (i, j, 0))
    ln = jnp.zeros((8, d), F32).at[0].set(ln_g).at[1].set(ln_b)
    return pl.pallas_call(
        _mix_out_kernel,
        name="mix_out",
        grid=(b, s // ts),
        in_specs=[tile, tile, tile,
                  pl.BlockSpec((1, 6, d), lambda i, j: (i, 0, 0)),
                  pl.BlockSpec((d, d), lambda i, j: (0, 0)),
                  pl.BlockSpec((8, d), lambda i, j: (0, 0))],
        out_specs=tile,
        out_shape=jax.ShapeDtypeStruct((b, s, d), F32),
        compiler_params=_params(("arbitrary", "arbitrary")),
    )(y, g, x, mod_l, w_o.astype(BF16), ln)


def _pool_kernel(x_ref, mod_ref, win_ref, wgrp_ref, sc_ref, wout_ref, ln_ref, o_ref, zext_ref, *, ts):
    s = pl.program_id(1)
    m = mod_ref[0]
    shift, scale, gate = m[0:1], m[1:2], m[2:3]
    x = x_ref[0]
    h = x * (1.0 + scale) + shift
    z = _dot(h.astype(BF16), win_ref[...])

    @pl.when(s == 0)
    def _():
        zext_ref[0:POOL_HALO, :] = jnp.zeros((POOL_HALO, z.shape[1]), F32)

    zext_ref[POOL_HALO:POOL_HALO + ts, :] = z
    pos = s * ts + lax.broadcasted_iota(jnp.int32, (ts, 1), 0)
    gw = z.shape[1] // len(POOL_WINDOWS)
    parts = []
    for gi, win in enumerate(POOL_WINDOWS):
        lo = gi * gw
        zg = z[:, lo:lo + gw]
        acc = zg
        for back in range(1, win):
            acc = acc + zext_ref[POOL_HALO - back:POOL_HALO - back + ts, lo:lo + gw]
        cnt = jnp.minimum(pos + 1, win).astype(F32)
        p = acc / cnt - zg
        parts.append(_dot(p.astype(BF16), wgrp_ref[gi]))
    y = jnp.concatenate(parts, axis=1) * sc_ref[...]
    o = _dot(y.astype(BF16), wout_ref[...])
    zext_ref[0:POOL_HALO, :] = zext_ref[ts:ts + POOL_HALO, :]
    ln = ln_ref[...]
    o_ref[0] = _layer_norm_rows(ALPHA * x + gate * o, ln[0:1], ln[1:2])


def _pool_mix(x, mod_l, w_in, w_grp, scale, w_out, ln_g, ln_b, ts):
    b, s, d = x.shape
    ng, gw, _ = w_grp.shape
    tile = pl.BlockSpec((1, ts, d), lambda i, j: (i, j, 0))
    ln = jnp.zeros((8, d), F32).at[0].set(ln_g).at[1].set(ln_b)
    return pl.pallas_call(
        functools.partial(_pool_kernel, ts=ts),
        name="pool_mix",
        grid=(b, s // ts),
        in_specs=[tile,
                  pl.BlockSpec((1, 6, d), lambda i, j: (i, 0, 0)),
                  pl.BlockSpec((d, d), lambda i, j: (0, 0)),
                  pl.BlockSpec((ng, gw, gw), lambda i, j: (0, 0, 0)),
                  pl.BlockSpec((1, d), lambda i, j: (0, 0)),
                  pl.BlockSpec((d, d), lambda i, j: (0, 0)),
                  pl.BlockSpec((8, d), lambda i, j: (0, 0))],
        out_specs=tile,
        out_shape=jax.ShapeDtypeStruct((b, s, d), F32),
        scratch_shapes=[pltpu.VMEM((ts + POOL_HALO, d), F32)],
        compiler_params=_params(("arbitrary", "arbitrary")),
    )(x, mod_l, w_in.astype(BF16), w_grp.astype(BF16), scale.reshape(1, d), w_out.astype(BF16), ln)


def _fold_kernel(keys_ref, wq_ref, o_ref):
    o_ref[0] = lax.dot_general(keys_ref[0, 0], wq_ref[0], NT_DIMS, precision=HI,
                               preferred_element_type=F32).astype(BF16)


def _peer_fold(keys, w_q):
    depth, nh, two, nk, dk = keys.shape
    d = w_q.shape[1]
    nb = nh * two
    return pl.pallas_call(
        _fold_kernel,
        name="peer_fold",
        grid=(depth, nb),
        in_specs=[pl.BlockSpec((1, 1, nk, dk), lambda l, j: (l, j, 0, 0)),
                  pl.BlockSpec((1, d, dk), lambda l, j: (l, 0, j))],
        out_specs=pl.BlockSpec((1, nk, d), lambda l, j: (l, j, 0)),
        out_shape=jax.ShapeDtypeStruct((depth, nb * nk, d), BF16),
        compiler_params=_params(("arbitrary", "arbitrary")),
    )(keys.reshape(depth, nb, nk, dk), w_q)


N_TOP = PEER_TOPK + 1
CAND = [(a, b) for a in range(N_TOP) for b in range(N_TOP) if (a + 1) * (b + 1) <= N_TOP]
CAND_ROWS = -(-len(CAND) // 8) * 8
TOP_ROWS = -(-N_TOP // 8) * 8


def _top_values(cur, out_ref, n):
    row = lax.broadcasted_iota(jnp.int32, cur.shape, 0)

    def body(i, cur):
        mx = jnp.max(cur, axis=0, keepdims=True)
        out_ref[pl.ds(i, 1), :] = mx
        first = jnp.min(jnp.where(cur == mx, row, cur.shape[0]), axis=0, keepdims=True)
        return jnp.where(row == first, NEG_INF, cur)

    lax.fori_loop(0, n, body, cur)


def _route_kernel(x_ref, mod_ref, wf_ref, h2_ref, s2p_ref, del_ref, th_ref, ta_ref, tb_ref, tc_ref):
    hd = pl.program_id(1)
    m = mod_ref[0]
    shift, scale = m[3:4], m[4:5]

    @pl.when(hd == 0)
    def _():
        h2_ref[...] = (x_ref[...] * (1.0 + scale) + shift).astype(BF16)

    h2 = h2_ref[...]
    s1 = _dot_nt(wf_ref[0:N_KEYS, :], h2)
    s2 = _dot_nt(wf_ref[N_KEYS:2 * N_KEYS, :], h2)
    _top_values(s1, ta_ref, N_TOP)
    _top_values(s2, tb_ref, N_TOP)
    ta = ta_ref[...]
    tb = tb_ref[...]
    rows = [ta[a:a + 1] + tb[b:b + 1] for a, b in CAND]
    rows += [jnp.full_like(rows[0], NEG_INF)] * (CAND_ROWS - len(CAND))
    cand = jnp.concatenate(rows, axis=0)
    _top_values(cand, tc_ref, N_TOP)
    tcv = tc_ref[...]
    tau = 0.5 * (tcv[PEER_TOPK - 1:PEER_TOPK] + tcv[PEER_TOPK:PEER_TOPK + 1])
    top = ta[0:1] + tb[0:1]
    zsum = jnp.sum(jnp.where(cand >= tau, jnp.exp(cand - top), 0.0), axis=0, keepdims=True)
    logz = jnp.log(zsum)
    s2p_ref[0] = (s2 - tb[0:1] - logz) * LOG2E
    del_ref[0] = (s1 - ta[0:1]) * LOG2E
    th_ref[0] = jnp.broadcast_to((tau - top - logz) * LOG2E, th_ref.shape[1:])


def _peer_route(x2d, mod_l, wf, tt, tiles_per_batch):
    t, d = x2d.shape
    nh = PEER_HEADS
    return pl.pallas_call(
        _route_kernel,
        name="peer_route",
        grid=(t // tt, nh),
        in_specs=[pl.BlockSpec((tt, d), lambda i, h: (i, 0)),
                  pl.BlockSpec((1, 6, d), lambda i, h: (i // tiles_per_batch, 0, 0)),
                  pl.BlockSpec((2 * N_KEYS, d), lambda i, h: (h, 0))],
        out_specs=[pl.BlockSpec((tt, d), lambda i, h: (i, 0)),
                   pl.BlockSpec((1, N_KEYS, tt), lambda i, h: (h, 0, i)),
                   pl.BlockSpec((1, N_KEYS, tt), lambda i, h: (h, 0, i)),
                   pl.BlockSpec((1, SUB, tt), lambda i, h: (h, 0, i))],
        out_shape=[jax.ShapeDtypeStruct((t, d), BF16),
                   jax.ShapeDtypeStruct((nh, N_KEYS, t), F32),
                   jax.ShapeDtypeStruct((nh, N_KEYS, t), F32),
                   jax.ShapeDtypeStruct((nh, SUB, t), F32)],
        scratch_shapes=[pltpu.VMEM((TOP_ROWS, tt), F32),
                        pltpu.VMEM((TOP_ROWS, tt), F32),
                        pltpu.VMEM((TOP_ROWS, tt), F32)],
        compiler_params=_params(("arbitrary", "arbitrary")),
    )(x2d, mod_l, wf)


ROWS = 16


def _dense_kernel(h2_ref, s2p_ref, del_ref, th_ref, u_ref, vt_ref, x_ref, mod_ref, ln_ref, o_ref,
                  acc_ref, z_ref, p_ref, *, eb):
    e = pl.program_id(1)
    tt = h2_ref.shape[0]
    half = eb // 2
    slabs = half // N_KEYS

    @pl.when(e == 0)
    def _():
        acc_ref[...] = jnp.zeros_like(acc_ref)

    h2 = h2_ref[...]
    for hf in range(2):
        z_ref[hf * half:(hf + 1) * half, :] = _dot_nt(u_ref[hf * half:(hf + 1) * half, :], h2)
    outs = []
    for hf in range(2):
        for il in range(slabs):
            slab = hf * slabs + il
            i1 = e * (eb // N_KEYS) + slab
            dlt = [jnp.broadcast_to(del_ref[hd, pl.ds(i1, 1), :], (SUB, tt)) for hd in range(PEER_HEADS)]
            for jb in range(N_KEYS // ROWS):
                r0 = slab * N_KEYS + jb * ROWS
                zz = z_ref[r0:r0 + ROWS, :].reshape(ROWS // SUB, SUB, tt)
                act = 0.5 * zz * (1.0 + lax.erf(zz * INV_SQRT2))
                gsum = jnp.zeros_like(zz)
                for hd in range(PEER_HEADS):
                    s2 = s2p_ref[hd, jb * ROWS:(jb + 1) * ROWS, :].reshape(ROWS // SUB, SUB, tt)
                    xl = s2 + dlt[hd][None]
                    gsum = gsum + jnp.where(xl >= th_ref[hd][None], jnp.exp2(xl), 0.0)
                p_ref[r0:r0 + ROWS, :] = (act * gsum).reshape(ROWS, tt).astype(BF16)
        outs.append(_dot(vt_ref[:, hf * half:(hf + 1) * half], p_ref[hf * half:(hf + 1) * half, :]))
    acc_ref[...] += outs[0] + outs[1]

    @pl.when(e == pl.num_programs(1) - 1)
    def _():
        m = mod_ref[0]
        gate = m[5:6]
        ln = ln_ref[...]
        y = acc_ref[...].T
        o_ref[...] = _layer_norm_rows(ALPHA * x_ref[...] + gate * y, ln[0:1], ln[1:2])


def _peer_dense(h2, s2p, dlt, th, u_bf, vt_bf, x2d, mod_l, ln_g, ln_b, tt, eb, tiles_per_batch):
    t, d = x2d.shape
    ne = u_bf.shape[0]
    nh = PEER_HEADS
    ln = jnp.zeros((8, d), F32).at[0].set(ln_g).at[1].set(ln_b)
    return pl.pallas_call(
        functools.partial(_dense_kernel, eb=eb),
        name="peer_dense",
        grid=(t // tt, ne // eb),
        in_specs=[pl.BlockSpec((tt, d), lambda i, e: (i, 0)),
                  pl.BlockSpec((nh, N_KEYS, tt), lambda i, e: (0, 0, i)),
                  pl.BlockSpec((nh, N_KEYS, tt), lambda i, e: (0, 0, i)),
                  pl.BlockSpec((nh, SUB, tt), lambda i, e: (0, 0, i)),
                  pl.BlockSpec((eb, d), lambda i, e: (e, 0)),
                  pl.BlockSpec((d, eb), lambda i, e: (0, e)),
                  pl.BlockSpec((tt, d), lambda i, e: (i, 0)),
                  pl.BlockSpec((1, 6, d), lambda i, e: (i // tiles_per_batch, 0, 0)),
                  pl.BlockSpec((8, d), lambda i, e: (0, 0))],
        out_specs=pl.BlockSpec((tt, d), lambda i, e: (i, 0)),
        out_shape=jax.ShapeDtypeStruct((t, d), F32),
        scratch_shapes=[pltpu.VMEM((d, tt), F32),
                        pltpu.VMEM((eb, tt), F32),
                        pltpu.VMEM((eb, tt), BF16)],
        compiler_params=_params(("arbitrary", "arbitrary")),
    )(h2, s2p, dlt, th, u_bf, vt_bf, x2d, mod_l, ln)


def _peer_ffn(x, mod_l, wf, u_tab, v_tab, ln_g, ln_b, tt, eb):
    b, s, d = x.shape
    x2d = x.reshape(b * s, d)
    tpb = s // tt
    h2, s2p, dlt, th = _peer_route(x2d, mod_l, wf, tt, tpb)
    out = _peer_dense(h2, s2p, dlt, th, u_tab.astype(BF16), v_tab.astype(BF16).T, x2d, mod_l,
                      ln_g, ln_b, tt, eb, tpb)
    return out.reshape(b, s, d)


def kernel(x, c, ada_w, ada_b, ln_g, ln_b, rw_mu, rw_w_rkv, rw_w0, rw_w1, rw_w2, rw_a0, rw_a1, rw_a2, rw_g1, rw_g2, rw_k_k, rw_k_a, rw_r_k, rw_lnx_g, rw_lnx_b, rw_w_o, pl_w_in, pl_w_grp, pl_scale, pl_w_out, pe_w_q, pe_keys, pe_u, pe_v):
    b, s, d = x.shape
    depth = ada_w.shape[0]
    ts = min(256, s)
    tc = min(512, s)
    tt = min(512, s)
    eb = 512
    mod = _adaln_mod(c, ada_w, ada_b)
    wf = _peer_fold(pe_keys, pe_w_q)
    for i in range(depth):
        j = i // 2
        if i % 2 == 0:
            r, lw, k, v, kkr, a, g = _rwkv_proj(x, mod[i], rw_mu[j], rw_w_rkv[j], rw_w0[j], rw_w1[j], rw_w2[j],
                                                rw_a0[j], rw_a1[j], rw_a2[j], rw_g1[j], rw_g2[j],
                                                rw_k_k[j], rw_k_a[j], ts)
            y = _wkv_scan(r, lw, k, v, kkr, a, rw_r_k[j], rw_lnx_g[j], rw_lnx_b[j], tc)
            x = _mix_out(y, g, x, mod[i], rw_w_o[j], ln_g[i, 0], ln_b[i, 0], ts)
        else:
            x = _pool_mix(x, mod[i], pl_w_in[j], pl_w_grp[j], pl_scale[j], pl_w_out[j],
                          ln_g[i, 0], ln_b[i, 0], ts)
        x = _peer_ffn(x, mod[i], wf[i], pe_u[i], pe_v[i], ln_g[i, 1], ln_b[i, 1], tt, eb)
    return x
```

```python
import functools
import math

import jax
import jax.numpy as jnp
from jax import lax
from jax.experimental import pallas as pl
from jax.experimental.pallas import tpu as pltpu

F32 = jnp.float32
BF16 = jnp.bfloat16
HI = lax.Precision.HIGHEST

HEAD = 64
PAIR = 2 * HEAD
SUB = 8
CHUNK = 64
GN_EPS = 64e-5
LN_EPS = 1e-5
DEPTH = 2
ALPHA = (2 * DEPTH) ** 0.25
POOL_WINDOWS = (2, 4, 8, 16)
POOL_HALO = 16
N_KEYS = 128
PEER_HEADS = 8
PEER_TOPK = 16
INV_SQRT2 = 0.7071067811865476
NEG_INF = float("-inf")
V7X_VMEM_LIMIT = 56 * 1024 * 1024

NT_DIMS = (((1,), (1,)), ((), ()))
TN_DIMS = (((0,), (0,)), ((), ()))


def _dot(a, b, precision=None):
    return jnp.dot(a, b, precision=precision, preferred_element_type=F32)


def _dot_nt(a, b):
    return lax.dot_general(a, b, NT_DIMS, preferred_element_type=F32)


def _dot_tn(a, b):
    return lax.dot_general(a, b, TN_DIMS, preferred_element_type=F32)


def _params(sem, vmem=V7X_VMEM_LIMIT, flags=None):
    return pltpu.CompilerParams(dimension_semantics=sem, vmem_limit_bytes=vmem, flags=flags)


def _layer_norm_rows(z, g, b):
    mu = jnp.mean(z, axis=-1, keepdims=True)
    d = z - mu
    var = jnp.mean(d * d, axis=-1, keepdims=True)
    return d * lax.rsqrt(var + LN_EPS) * g + b


def _mod_kernel(c_ref, w_ref, b_ref, o_ref):
    c = c_ref[...]
    cond = c * jax.nn.sigmoid(c)
    o_ref[0] = _dot(cond, w_ref[0], HI) + b_ref[0]


def _adaln_mod(c, ada_w, ada_b):
    depth, d, nd = ada_w.shape
    b = c.shape[0]
    nmod = nd // d
    out = pl.pallas_call(
        _mod_kernel,
        name="adaln_mod",
        grid=(depth, nmod),
        in_specs=[
            pl.BlockSpec((b, d), lambda l, n: (0, 0)),
            pl.BlockSpec((1, d, d), lambda l, n: (l, 0, n)),
            pl.BlockSpec((1, 1, d), lambda l, n: (l, 0, n)),
        ],
        out_specs=pl.BlockSpec((1, b, d), lambda l, n: (l, 0, n)),
        out_shape=jax.ShapeDtypeStruct((depth, b, nd), F32),
        compiler_params=_params(("arbitrary", "arbitrary")),
    )(c, ada_w, ada_b.reshape(depth, 1, nd))
    return out.reshape(depth, b, nmod, d)


def _softplus(z):
    return jnp.maximum(z, 0.0) + jnp.log1p(jnp.exp(-jnp.abs(z)))


def _rwkv_proj_kernel(x_ref, xp_ref, mod_ref, mu_ref, vec_ref, wrkv_ref, w1_ref, w2_ref, a1_ref, a2_ref,
                      g1_ref, g2_ref, r_ref, lw_ref, k_ref, v_ref, kk_ref, a_ref, g_ref):
    s = pl.program_id(1)
    m = mod_ref[0]
    shift, scale = m[0:1], m[1:2]
    h = x_ref[0] * (1.0 + scale) + shift
    prev = xp_ref[0][7:8] * (1.0 + scale) + shift
    prev = jnp.where(s == 0, 0.0, prev)
    row = lax.broadcasted_iota(jnp.int32, h.shape, 0)
    hprev = jnp.where(row == 0, prev, pltpu.roll(h, 1, 0))
    xx = hprev - h
    mu = mu_ref[...]

    def mix(n):
        return (h + xx * mu[n:n + 1]).astype(BF16)

    vec = vec_ref[...]
    w0, a0, k_k, k_a = vec[0:1], vec[1:2], vec[2:3], vec[3:4]
    r = _dot(mix(0), wrkv_ref[0])
    k = _dot(mix(1), wrkv_ref[1])
    v = _dot(mix(2), wrkv_ref[2])
    wl = w0 + _dot(jnp.tanh(_dot(mix(3), w1_ref[...])).astype(BF16), w2_ref[...])
    w = -_softplus(-wl) - 0.5
    a = jax.nn.sigmoid(a0 + _dot(_dot(mix(4), a1_ref[...]).astype(BF16), a2_ref[...]))
    g = _dot(jax.nn.sigmoid(_dot(mix(5), g1_ref[...])).astype(BF16), g2_ref[...])
    r_ref[0] = r
    lw_ref[0] = -jnp.exp(w)
    k_ref[0] = k * (1.0 + (a - 1.0) * k_a)
    v_ref[0] = v
    kk_ref[0] = k * k_k
    a_ref[0] = a
    g_ref[0] = g.astype(BF16)


def _pad_cols(w, n):
    return jnp.pad(w, ((0, 0), (0, n - w.shape[1])))


def _pad_rows(w, n):
    return jnp.pad(w, ((0, n - w.shape[0]), (0, 0)))


def _rwkv_proj(x, mod_l, mu, w_rkv, w0, w1, w2, a0, a1, a2, g1, g2, k_k, k_a, ts):
    b, s, d = x.shape
    lora = 128
    glora = 256
    vec = jnp.zeros((8, d), F32).at[0].set(w0).at[1].set(a0).at[2].set(k_k).at[3].set(k_a)
    tile = pl.BlockSpec((1, ts, d), lambda i, j: (i, j, 0))
    full2 = lambda shape: pl.BlockSpec(shape, lambda i, j: (0, 0))
    outs = pl.pallas_call(
        _rwkv_proj_kernel,
        name="rwkv_proj",
        grid=(b, s // ts),
        in_specs=[
            tile,
            pl.BlockSpec((1, 8, d), lambda i, j: (i, jnp.maximum(j * (ts // 8) - 1, 0), 0)),
            pl.BlockSpec((1, 6, d), lambda i, j: (i, 0, 0)),
            full2((6, d)),
            full2((8, d)),
            pl.BlockSpec((3, d, d), lambda i, j: (0, 0, 0)),
            full2((d, lora)), full2((lora, d)),
            full2((d, lora)), full2((lora, d)),
            full2((d, glora)), full2((glora, d)),
        ],
        out_specs=[tile] * 7,
        out_shape=[jax.ShapeDtypeStruct((b, s, d), F32)] * 6 + [jax.ShapeDtypeStruct((b, s, d), BF16)],
        compiler_params=_params(("arbitrary", "arbitrary")),
    )(x, x, mod_l, mu, vec, w_rkv.astype(BF16),
      _pad_cols(w1, lora).astype(BF16), _pad_rows(w2, lora).astype(BF16),
      _pad_cols(a1, lora).astype(BF16), _pad_rows(a2, lora).astype(BF16),
      _pad_cols(g1, glora).astype(BF16), _pad_rows(g2, glora).astype(BF16))
    return outs


def _head_masks(shape):
    lane = lax.broadcasted_iota(jnp.int32, shape, len(shape) - 1)
    first = (lane % PAIR) < HEAD
    return first, jnp.logical_not(first)


def _stack_heads(z):
    m0, m1 = _head_masks(z.shape)
    return jnp.concatenate([jnp.where(m0, z, 0.0), jnp.where(m1, z, 0.0)], axis=0)


def _wkv_kernel(r_ref, lw_ref, k_ref, v_ref, kk_ref, a_ref, rk_ref, lng_ref, lnb_ref, y_ref,
                s_ref, yb_ref, *, nchunk):
    L = CHUNK

    @pl.when(pl.program_id(2) == 0)
    def _():
        s_ref[...] = jnp.zeros_like(s_ref)

    r2 = lax.broadcasted_iota(jnp.int32, (PAIR, PAIR), 0)
    c2 = lax.broadcasted_iota(jnp.int32, (PAIR, PAIR), 1)
    same_head = (r2 // HEAD) == (c2 // HEAD)
    eye = r2 == c2
    bd_ones = jnp.where(same_head, 1.0, 0.0).astype(F32)
    tr = lax.broadcasted_iota(jnp.int32, (L, PAIR), 0)
    tc = lax.broadcasted_iota(jnp.int32, (L, PAIR), 1) % HEAD
    strict = tr > tc
    incl = tr >= tc

    def off_diag(size):
        same = (tr // (2 * size)) == (tc // (2 * size))
        return same & ((tr % (2 * size)) >= size) & ((tc % (2 * size)) < size)
    lr = lax.broadcasted_iota(jnp.int32, (L, L), 0)
    lc = lax.broadcasted_iota(jnp.int32, (L, L), 1)
    ltri = jnp.where(lr >= lc, 1.0, 0.0).astype(F32)

    chunks = range(nchunk)
    r_all = r_ref[0]
    lw_all = lw_ref[0]
    k_all = k_ref[0]
    v_all = v_ref[0]
    a_all = a_ref[0]
    kkr = kk_ref[0]
    kk = kkr * lax.rsqrt(jnp.maximum(_dot(kkr * kkr, bd_ones, HI), 1e-24))
    bv_all = kk * a_all

    def rows(x, c):
        return x[c * L:(c + 1) * L]

    cs = [_dot(ltri, rows(lw_all, c), HI) for c in chunks]
    rt, at, vst, lhs, rhs, bc, kc, p_last = [], [], [], [], [], [], [], []
    for c in chunks:
        cs_last = cs[c][L - 1:L, :]
        pinv = jnp.exp(-cs[c])
        prem = jnp.exp(cs_last - cs[c])
        rt.append(rows(r_all, c) * jnp.exp(cs[c]))
        at.append(-rows(kk, c) * jnp.exp(cs[c] - rows(lw_all, c)))
        bt = rows(bv_all, c) * pinv
        kt = rows(k_all, c) * pinv
        bc.append((rows(bv_all, c) * prem).astype(BF16))
        kc.append((rows(k_all, c) * prem).astype(BF16))
        p_last.append(jnp.exp(cs_last))
        vst.append(_stack_heads(rows(v_all, c)).astype(BF16))
        lhs.append(jnp.concatenate([at[c], rt[c]], axis=0).astype(BF16))
        rhs.append(jnp.concatenate([_stack_heads(bt), _stack_heads(kt)], axis=0).astype(BF16))
    o = [_dot_nt(lhs[c], rhs[c]) for c in chunks]
    nmat = [jnp.where(strict, o[c][:L, :PAIR], 0.0) for c in chunks]
    akv = [_dot(jnp.where(strict, o[c][:L, PAIR:], 0.0).astype(BF16), vst[c]) for c in chunks]
    ident = jnp.where(tr == tc, 1.0, 0.0)
    tinv = [ident + jnp.where(off_diag(1), nmat[c], 0.0) for c in chunks]
    size = 2
    while size < L:
        mask = off_diag(size)
        tn = [_dot(tinv[c].astype(BF16), _stack_heads(jnp.where(mask, nmat[c], 0.0)).astype(BF16)) for c in chunks]
        tinv = [tinv[c] + _dot(tn[c].astype(BF16), _stack_heads(tinv[c]).astype(BF16)) for c in chunks]
        size *= 2
    z = [_dot(tinv[c].astype(BF16), _stack_heads(jnp.concatenate([at[c], akv[c]], axis=1)).astype(BF16))
         for c in chunks]
    tmp = [_dot(jnp.where(incl, o[c][L:, :PAIR], 0.0).astype(BF16), _stack_heads(z[c]).astype(BF16))
           for c in chunks]
    rkv = [_dot(jnp.where(incl, o[c][L:, PAIR:], 0.0).astype(BF16), vst[c]) for c in chunks]
    bz = [_dot_tn(bc[c], z[c].astype(BF16)) for c in chunks]
    kv = [_dot_tn(kc[c], rows(v_all, c).astype(BF16)) for c in chunks]
    state = s_ref[...]
    for c in chunks:
        sb = state.astype(BF16)
        rh = (rt[c] + tmp[c][:, :PAIR]).astype(BF16)
        yb_ref[c * L:(c + 1) * L, :] = _dot(rh, sb) + tmp[c][:, PAIR:] + rkv[c]
        m_c = jnp.where(eye, jnp.broadcast_to(p_last[c], (PAIR, PAIR)), 0.0) + jnp.where(same_head, bz[c][:, :PAIR], 0.0)
        state = _dot(m_c.astype(BF16), sb) + jnp.where(same_head, bz[c][:, PAIR:] + kv[c], 0.0)
    s_ref[...] = state

    y = yb_ref[...]
    bd_avg = bd_ones * (1.0 / HEAD)
    ym = _dot(y, bd_avg, HI)
    d = y - ym
    yv = _dot(d * d, bd_avg, HI)
    yn = d * lax.rsqrt(yv + GN_EPS) * lng_ref[...] + lnb_ref[...]
    bonus = _dot(r_all * k_all * rk_ref[...], bd_ones, HI) * v_all
    y_ref[0] = yn + bonus


def _wkv_scan(r, lw, k, v, kkr, a, r_k, lnx_g, lnx_b, tc):
    b, s, d = r.shape
    nchunk = tc // CHUNK
    tile = pl.BlockSpec((1, tc, PAIR), lambda i, p, j: (i, j, p))
    row = pl.BlockSpec((1, PAIR), lambda i, p, j: (0, p))
    return pl.pallas_call(
        functools.partial(_wkv_kernel, nchunk=nchunk),
        name="wkv_scan",
        grid=(b, d // PAIR, s // tc),
        in_specs=[tile] * 6 + [row] * 3,
        out_specs=tile,
        out_shape=jax.ShapeDtypeStruct((b, s, d), F32),
        scratch_shapes=[
            pltpu.VMEM((PAIR, PAIR), F32),
            pltpu.VMEM((tc, PAIR), F32),
        ],
        compiler_params=_params(("arbitrary", "arbitrary", "arbitrary")),
    )(r, lw, k, v, kkr, a, r_k.reshape(1, d), lnx_g.reshape(1, d), lnx_b.reshape(1, d))


def _mix_out_kernel(y_ref, g_ref, x_ref, mod_ref, wo_ref, ln_ref, o_ref):
    m = mod_ref[0]
    gate = m[2:3]
    yg = (y_ref[0] * g_ref[0].astype(F32)).astype(BF16)
    o = _dot(yg, wo_ref[...])
    z = ALPHA * x_ref[0] + gate * o
    ln = ln_ref[...]
    o_ref[0] = _layer_norm_rows(z, ln[0:1], ln[1:2])


def _mix_out(y, g, x, mod_l, w_o, ln_g, ln_b, ts):
    b, s, d = x.shape
    tile = pl.BlockSpec((1, ts, d), lambda i, j: (i, j, 0))
    ln = jnp.zeros((8, d), F32).at[0].set(ln_g).at[1].set(ln_b)
    return pl.pallas_call(
        _mix_out_kernel,
        name="mix_out",
        grid=(b, s // ts),
        in_specs=[tile, tile, tile,
                  pl.BlockSpec((1, 6, d), lambda i, j: (i, 0, 0)),
                  pl.BlockSpec((d, d), lambda i, j: (0, 0)),
                  pl.BlockSpec((8, d), lambda i, j: (0, 0))],
        out_specs=tile,
        out_shape=jax.ShapeDtypeStruct((b, s, d), F32),
        compiler_params=_params(("arbitrary", "arbitrary")),
    )(y, g, x, mod_l, w_o.astype(BF16), ln)


def _pool_kernel(x_ref, mod_ref, win_ref, wgrp_ref, sc_ref, wout_ref, ln_ref, o_ref, zext_ref, *, ts):
    s = pl.program_id(1)
    m = mod_ref[0]
    shift, scale, gate = m[0:1], m[1:2], m[2:3]
    x = x_ref[0]
    h = x * (1.0 + scale) + shift
    z = _dot(h.astype(BF16), win_ref[...])

    @pl.when(s == 0)
    def _():
        zext_ref[0:POOL_HALO, :] = jnp.zeros((POOL_HALO, z.shape[1]), F32)

    zext_ref[POOL_HALO:POOL_HALO + ts, :] = z
    pos = s * ts + lax.broadcasted_iota(jnp.int32, (ts, 1), 0)
    gw = z.shape[1] // len(POOL_WINDOWS)
    parts = []
    for gi, win in enumerate(POOL_WINDOWS):
        lo = gi * gw
        zg = z[:, lo:lo + gw]
        acc = zg
        for back in range(1, win):
            acc = acc + zext_ref[POOL_HALO - back:POOL_HALO - back + ts, lo:lo + gw]
        cnt = jnp.minimum(pos + 1, win).astype(F32)
        p = acc / cnt - zg
        parts.append(_dot(p.astype(BF16), wgrp_ref[gi]))
    y = jnp.concatenate(parts, axis=1) * sc_ref[...]
    o = _dot(y.astype(BF16), wout_ref[...])
    zext_ref[0:POOL_HALO, :] = zext_ref[ts:ts + POOL_HALO, :]
    ln = ln_ref[...]
    o_ref[0] = _layer_norm_rows(ALPHA * x + gate * o, ln[0:1], ln[1:2])


def _pool_mix(x, mod_l, w_in, w_grp, scale, w_out, ln_g, ln_b, ts):
    b, s, d = x.shape
    ng, gw, _ = w_grp.shape
    tile = pl.BlockSpec((1, ts, d), lambda i, j: (i, j, 0))
    ln = jnp.zeros((8, d), F32).at[0].set(ln_g).at[1].set(ln_b)
    return pl.pallas_call(
        functools.partial(_pool_kernel, ts=ts),
        name="pool_mix",
        grid=(b, s // ts),
        in_specs=[tile,
                  pl.BlockSpec((1, 6, d), lambda i, j: (i, 0, 0)),
                  pl.BlockSpec((d, d), lambda i, j: (0, 0)),
                  pl.BlockSpec((ng, gw, gw), lambda i, j: (0, 0, 0)),
                  pl.BlockSpec((1, d), lambda i, j: (0, 0)),
                  pl.BlockSpec((d, d), lambda i, j: (0, 0)),
                  pl.BlockSpec((8, d), lambda i, j: (0, 0))],
        out_specs=tile,
        out_shape=jax.ShapeDtypeStruct((b, s, d), F32),
        scratch_shapes=[pltpu.VMEM((ts + POOL_HALO, d), F32)],
        compiler_params=_params(("arbitrary", "arbitrary")),
    )(x, mod_l, w_in.astype(BF16), w_grp.astype(BF16), scale.reshape(1, d), w_out.astype(BF16), ln)


def _fold_kernel(keys_ref, wq_ref, o_ref):
    o_ref[0] = lax.dot_general(keys_ref[0, 0], wq_ref[0], NT_DIMS, precision=HI,
                               preferred_element_type=F32).astype(BF16)


def _peer_fold(keys, w_q):
    depth, nh, two, nk, dk = keys.shape
    d = w_q.shape[1]
    nb = nh * two
    return pl.pallas_call(
        _fold_kernel,
        name="peer_fold",
        grid=(depth, nb),
        in_specs=[pl.BlockSpec((1, 1, nk, dk), lambda l, j: (l, j, 0, 0)),
                  pl.BlockSpec((1, d, dk), lambda l, j: (l, 0, j))],
        out_specs=pl.BlockSpec((1, nk, d), lambda l, j: (l, j, 0)),
        out_shape=jax.ShapeDtypeStruct((depth, nb * nk, d), BF16),
        compiler_params=_params(("arbitrary", "arbitrary")),
    )(keys.reshape(depth, nb, nk, dk), w_q)


N_TOP = PEER_TOPK + 1
CAND = [(a, b) for a in range(N_TOP) for b in range(N_TOP) if (a + 1) * (b + 1) <= N_TOP]
CAND_ROWS = -(-len(CAND) // SUB) * SUB
TOP_ROWS = -(-N_TOP // SUB) * SUB
KEY_GROUPS = N_KEYS // SUB


def _sorting_network(n):
    pairs = []

    def merge(lo, cnt, step):
        nxt = step * 2
        if nxt < cnt:
            merge(lo, cnt, nxt)
            merge(lo + step, cnt, nxt)
            for i in range(lo + step, lo + cnt - step, nxt):
                pairs.append((i, i + step))
        else:
            pairs.append((lo, lo + step))

    def sort(lo, cnt):
        if cnt > 1:
            half = cnt // 2
            sort(lo, half)
            sort(lo + half, half)
            merge(lo, cnt, 1)

    sort(0, n)
    return pairs


GROUP_SORT = _sorting_network(KEY_GROUPS)


def _top_values(cur, out_ref, n):
    row = lax.broadcasted_iota(jnp.int32, cur.shape, 0)

    def body(i, cur):
        mx = jnp.max(cur, axis=0, keepdims=True)
        out_ref[pl.ds(i, 1), :] = mx
        first = jnp.min(jnp.where(cur == mx, row, cur.shape[0]), axis=0, keepdims=True)
        return jnp.where(row == first, NEG_INF, cur)

    lax.fori_loop(0, n, body, cur)


def _top_sorted(s, n):
    cols = [s[g * SUB:(g + 1) * SUB, :] for g in range(KEY_GROUPS)]
    for i, j in GROUP_SORT:
        cols[i], cols[j] = jnp.maximum(cols[i], cols[j]), jnp.minimum(cols[i], cols[j])
    sub = lax.broadcasted_iota(jnp.int32, cols[0].shape, 0)
    outs = []
    for it in range(n):
        mx = jnp.max(cols[0], axis=0, keepdims=True)
        outs.append(mx)
        left = n - 1 - it
        if left == 0:
            break
        first = jnp.min(jnp.where(cols[0] == mx, sub, SUB), axis=0, keepdims=True)
        pop = sub == first
        for lvl in range(min(KEY_GROUPS, left)):
            below = cols[lvl + 1] if lvl + 1 < KEY_GROUPS else NEG_INF
            cols[lvl] = jnp.where(pop, below, cols[lvl])
    return outs


def _route_kernel(x_ref, mod_ref, wf_ref, h2_ref, e2_ref, rk_ref, c1_ref, n1_ref, tc_ref):
    hd = pl.program_id(1)
    m = mod_ref[0]
    shift, scale = m[3:4], m[4:5]

    @pl.when(hd == 0)
    def _():
        h2_ref[...] = (x_ref[...] * (1.0 + scale) + shift).astype(BF16)

    h2 = h2_ref[...]
    s1 = _dot_nt(wf_ref[0:N_KEYS, :], h2)
    s2 = _dot_nt(wf_ref[N_KEYS:2 * N_KEYS, :], h2)
    ta = _top_sorted(s1, N_TOP)
    tb = _top_sorted(s2, N_TOP)
    rows = [ta[a] + tb[b] for a, b in CAND]
    rows += [jnp.full_like(rows[0], NEG_INF)] * (CAND_ROWS - len(CAND))
    cand = jnp.concatenate(rows, axis=0)
    _top_values(cand, tc_ref, N_TOP)
    tcv = tc_ref[...]
    tau = 0.5 * (tcv[PEER_TOPK - 1:PEER_TOPK] + tcv[PEER_TOPK:PEER_TOPK + 1])
    top = ta[0] + tb[0]
    zsum = jnp.sum(jnp.where(cand >= tau, jnp.exp(cand - top), 0.0), axis=0, keepdims=True)
    need = tau - s1
    rank2 = jnp.zeros_like(s2)
    count1 = jnp.zeros_like(s1)
    for b in range(PEER_TOPK):
        rank2 = rank2 + jnp.where(tb[b] > s2, 1.0, 0.0)
        count1 = count1 + jnp.where(tb[b] >= need, 1.0, 0.0)
    e2_ref[0] = jnp.exp(s2 - tb[0]).astype(BF16)
    rk_ref[0] = rank2.astype(BF16)
    c1_ref[0] = jnp.exp(s1 - ta[0]) / zsum
    n1_ref[0] = count1


def _peer_route(x2d, mod_l, wf, tt, tiles_per_batch):
    t, d = x2d.shape
    nh = PEER_HEADS
    per_head = pl.BlockSpec((1, N_KEYS, tt), lambda i, h: (h, 0, i))
    return pl.pallas_call(
        _route_kernel,
        name="peer_route",
        grid=(t // tt, nh),
        in_specs=[pl.BlockSpec((tt, d), lambda i, h: (i, 0)),
                  pl.BlockSpec((1, 6, d), lambda i, h: (i // tiles_per_batch, 0, 0)),
                  pl.BlockSpec((2 * N_KEYS, d), lambda i, h: (h, 0))],
        out_specs=[pl.BlockSpec((tt, d), lambda i, h: (i, 0)), per_head, per_head, per_head, per_head],
        out_shape=[jax.ShapeDtypeStruct((t, d), BF16),
                   jax.ShapeDtypeStruct((nh, N_KEYS, t), BF16),
                   jax.ShapeDtypeStruct((nh, N_KEYS, t), BF16),
                   jax.ShapeDtypeStruct((nh, N_KEYS, t), F32),
                   jax.ShapeDtypeStruct((nh, N_KEYS, t), F32)],
        scratch_shapes=[pltpu.VMEM((TOP_ROWS, tt), F32)],
        compiler_params=_params(("arbitrary", "arbitrary")),
    )(x2d, mod_l, wf)


ROWS = 16
TOKEN_COLS = 256
MXU_ROWS = 256
MXU_COLS = 256


def _dense_kernel(h2_ref, e2_ref, rk_ref, c1_ref, n1_ref, u_ref, vt_ref, x_ref, mod_ref, ln_ref, o_ref,
                  acc_ref, za_ref, zb_ref, pa_ref, pb_ref, *, eb):
    e = pl.program_id(1)
    last = pl.num_programs(1) - 1
    tt = h2_ref.shape[0]
    slabs = eb // N_KEYS

    @pl.when(e == 0)
    def _():
        acc_ref[...] = jnp.zeros_like(acc_ref)
        zb_ref[...] = jnp.zeros_like(zb_ref)
        pa_ref[...] = jnp.zeros_like(pa_ref)

    def gate_blocks(z_ref, p_ref):
        blocks = []
        for il in range(slabs):
            i1 = jnp.clip((e - 1) * slabs + il, 0, N_KEYS - 1)
            for t0 in range(0, tt, TOKEN_COLS):
                cols = slice(t0, t0 + TOKEN_COLS)
                rows_of = {}

                def token_rows(i1=i1, cols=cols, rows_of=rows_of):
                    if not rows_of:
                        rows_of["cnt"] = [jnp.broadcast_to(n1_ref[hd, pl.ds(i1, 1), cols], (ROWS, TOKEN_COLS)).astype(BF16)
                                          for hd in range(PEER_HEADS)]
                        rows_of["wgt"] = [jnp.broadcast_to(c1_ref[hd, pl.ds(i1, 1), cols], (ROWS, TOKEN_COLS)).astype(BF16)
                                          for hd in range(PEER_HEADS)]
                    return rows_of["cnt"], rows_of["wgt"]

                for jb in range(N_KEYS // ROWS):
                    def block(il=il, jb=jb, cols=cols, token_rows=token_rows):
                        cnt, wgt = token_rows()
                        r0 = il * N_KEYS + jb * ROWS
                        zz = z_ref[r0:r0 + ROWS, cols]
                        act = (0.5 * zz * (1.0 + lax.erf(zz * INV_SQRT2))).astype(BF16)
                        gsum = jnp.zeros((ROWS, TOKEN_COLS), BF16)
                        for hd in range(PEER_HEADS):
                            keep = rk_ref[hd, jb * ROWS:(jb + 1) * ROWS, cols] < cnt[hd]
                            gsum = gsum + jnp.where(keep, e2_ref[hd, jb * ROWS:(jb + 1) * ROWS, cols], 0.0) * wgt[hd]
                        p_ref[r0:r0 + ROWS, cols] = act * gsum
                    blocks.append(block)
        return blocks

    def matmul_units(z_new, p_old):
        units = []
        for n0 in range(0, tt, MXU_COLS):
            for m0 in range(0, acc_ref.shape[0], MXU_ROWS):
                def unit(m0=m0, n0=n0):
                    acc_ref[m0:m0 + MXU_ROWS, n0:n0 + MXU_COLS] += _dot(vt_ref[m0:m0 + MXU_ROWS, :], p_old[:, n0:n0 + MXU_COLS])
                units.append((eb, unit))
            for m0 in range(0, eb, MXU_ROWS):
                def unit(m0=m0, n0=n0):
                    z_new[m0:m0 + MXU_ROWS, n0:n0 + MXU_COLS] = _dot_nt(u_ref[m0:m0 + MXU_ROWS, :], h2_ref[n0:n0 + MXU_COLS, :])
                units.append((u_ref.shape[1], unit))
        return units

    def step(z_new, z_old, p_new, p_old):
        blocks = gate_blocks(z_old, p_new)
        units = matmul_units(z_new, p_old)
        total = sum(k for k, _ in units)
        done = 0
        issued = 0
        for k, unit in units:
            unit()
            done += k
            upto = -(-len(blocks) * done // total)
            for block in blocks[issued:upto]:
                block()
            issued = upto

    @pl.when(e % 2 == 0)
    def _():
        step(za_ref, zb_ref, pb_ref, pa_ref)

    @pl.when(e % 2 == 1)
    def _():
        step(zb_ref, za_ref, pa_ref, pb_ref)

    @pl.when(e == last)
    def _():
        m = mod_ref[0]
        gate = m[5:6]
        ln = ln_ref[...]
        y = acc_ref[...].T
        o_ref[...] = _layer_norm_rows(ALPHA * x_ref[...] + gate * y, ln[0:1], ln[1:2])


def _peer_dense(h2, e2, rk2, c1, n1, u_bf, vt_bf, x2d, mod_l, ln_g, ln_b, tt, eb, tiles_per_batch):
    t, d = x2d.shape
    ne = u_bf.shape[0]
    nh = PEER_HEADS
    ln = jnp.zeros((8, d), F32).at[0].set(ln_g).at[1].set(ln_b)
    per_tile = pl.BlockSpec((nh, N_KEYS, tt), lambda i, e: (0, 0, i))
    nslab = ne // eb
    return pl.pallas_call(
        functools.partial(_dense_kernel, eb=eb),
        name="peer_dense",
        grid=(t // tt, nslab + 2),
        in_specs=[pl.BlockSpec((tt, d), lambda i, e: (i, 0)),
                  per_tile, per_tile, per_tile, per_tile,
                  pl.BlockSpec((eb, d), lambda i, e: (jnp.minimum(e, nslab - 1), 0)),
                  pl.BlockSpec((d, eb), lambda i, e: (0, jnp.clip(e - 2, 0, nslab - 1))),
                  pl.BlockSpec((tt, d), lambda i, e: (i, 0)),
                  pl.BlockSpec((1, 6, d), lambda i, e: (i // tiles_per_batch, 0, 0)),
                  pl.BlockSpec((8, d), lambda i, e: (0, 0))],
        out_specs=pl.BlockSpec((tt, d), lambda i, e: (i, 0)),
        out_shape=jax.ShapeDtypeStruct((t, d), F32),
        scratch_shapes=[pltpu.VMEM((d, tt), F32),
                        pltpu.VMEM((eb, tt), F32),
                        pltpu.VMEM((eb, tt), F32),
                        pltpu.VMEM((eb, tt), BF16),
                        pltpu.VMEM((eb, tt), BF16)],
        compiler_params=_params(("arbitrary", "arbitrary")),
    )(h2, e2, rk2, c1, n1, u_bf, vt_bf, x2d, mod_l, ln)


def _peer_ffn(x, mod_l, wf, u_tab, v_tab, ln_g, ln_b, tt, eb):
    b, s, d = x.shape
    x2d = x.reshape(b * s, d)
    tpb = s // tt
    h2, e2, rk2, c1, n1 = _peer_route(x2d, mod_l, wf, tt, tpb)
    out = _peer_dense(h2, e2, rk2, c1, n1, u_tab.astype(BF16), v_tab.astype(BF16).T, x2d, mod_l,
                      ln_g, ln_b, tt, eb, tpb)
    return out.reshape(b, s, d)


def kernel(x, c, ada_w, ada_b, ln_g, ln_b, rw_mu, rw_w_rkv, rw_w0, rw_w1, rw_w2, rw_a0, rw_a1, rw_a2, rw_g1, rw_g2, rw_k_k, rw_k_a, rw_r_k, rw_lnx_g, rw_lnx_b, rw_w_o, pl_w_in, pl_w_grp, pl_scale, pl_w_out, pe_w_q, pe_keys, pe_u, pe_v):
    b, s, d = x.shape
    depth = ada_w.shape[0]
    ts = min(256, s)
    tc = min(512, s)
    tt = min(512, s)
    eb = 512
    mod = _adaln_mod(c, ada_w, ada_b)
    wf = _peer_fold(pe_keys, pe_w_q)
    for i in range(depth):
        j = i // 2
        if i % 2 == 0:
            r, lw, k, v, kkr, a, g = _rwkv_proj(x, mod[i], rw_mu[j], rw_w_rkv[j], rw_w0[j], rw_w1[j], rw_w2[j],
                                                rw_a0[j], rw_a1[j], rw_a2[j], rw_g1[j], rw_g2[j],
                                                rw_k_k[j], rw_k_a[j], ts)
            y = _wkv_scan(r, lw, k, v, kkr, a, rw_r_k[j], rw_lnx_g[j], rw_lnx_b[j], tc)
            x = _mix_out(y, g, x, mod[i], rw_w_o[j], ln_g[i, 0], ln_b[i, 0], ts)
        else:
            x = _pool_mix(x, mod[i], pl_w_in[j], pl_w_grp[j], pl_scale[j], pl_w_out[j],
                          ln_g[i, 0], ln_b[i, 0], ts)
        x = _peer_ffn(x, mod[i], wf[i], pe_u[i], pe_v[i], ln_g[i, 1], ln_b[i, 1], tt, eb)
    return x
```

```python
import functools
import math

import jax
import jax.numpy as jnp
from jax import lax
from jax.experimental import pallas as pl
from jax.experimental.pallas import tpu as pltpu

F32 = jnp.float32
BF16 = jnp.bfloat16
HI = lax.Precision.HIGHEST

HEAD = 64
PAIR = 2 * HEAD
SUB = 8
CHUNK = 64
GN_EPS = 64e-5
LN_EPS = 1e-5
DEPTH = 2
ALPHA = (2 * DEPTH) ** 0.25
POOL_WINDOWS = (2, 4, 8, 16)
POOL_HALO = 16
N_KEYS = 128
PEER_HEADS = 8
PEER_TOPK = 16
INV_SQRT2 = 0.7071067811865476
NEG_INF = float("-inf")
V7X_VMEM_LIMIT = 56 * 1024 * 1024

NT_DIMS = (((1,), (1,)), ((), ()))
TN_DIMS = (((0,), (0,)), ((), ()))


def _dot(a, b, precision=None):
    return jnp.dot(a, b, precision=precision, preferred_element_type=F32)


def _dot_nt(a, b):
    return lax.dot_general(a, b, NT_DIMS, preferred_element_type=F32)


def _dot_tn(a, b):
    return lax.dot_general(a, b, TN_DIMS, preferred_element_type=F32)


def _split_bf16(x):
    hi = x.astype(BF16)
    return hi, (x - hi.astype(F32)).astype(BF16)


def _dot_sel(x, sel):
    hi, lo = _split_bf16(x)
    sel = sel.astype(BF16)
    return _dot(hi, sel) + _dot(lo, sel)


def _sel_dot(sel, x):
    hi, lo = _split_bf16(x)
    sel = sel.astype(BF16)
    return _dot(sel, hi) + _dot(sel, lo)


def _params(sem, vmem=V7X_VMEM_LIMIT, flags=None):
    return pltpu.CompilerParams(dimension_semantics=sem, vmem_limit_bytes=vmem, flags=flags)


def _layer_norm_rows(z, g, b):
    mu = jnp.mean(z, axis=-1, keepdims=True)
    d = z - mu
    var = jnp.mean(d * d, axis=-1, keepdims=True)
    return d * lax.rsqrt(var + LN_EPS) * g + b


def _mod_kernel(c_ref, w_ref, b_ref, o_ref):
    c = c_ref[...]
    cond = c * jax.nn.sigmoid(c)
    o_ref[0] = _dot(cond, w_ref[0], HI) + b_ref[0]


def _adaln_mod(c, ada_w, ada_b):
    depth, d, nd = ada_w.shape
    b = c.shape[0]
    nmod = nd // d
    out = pl.pallas_call(
        _mod_kernel,
        name="adaln_mod",
        grid=(depth, nmod),
        in_specs=[
            pl.BlockSpec((b, d), lambda l, n: (0, 0)),
            pl.BlockSpec((1, d, d), lambda l, n: (l, 0, n)),
            pl.BlockSpec((1, 1, d), lambda l, n: (l, 0, n)),
        ],
        out_specs=pl.BlockSpec((1, b, d), lambda l, n: (l, 0, n)),
        out_shape=jax.ShapeDtypeStruct((depth, b, nd), F32),
        compiler_params=_params(("arbitrary", "arbitrary")),
    )(c, ada_w, ada_b.reshape(depth, 1, nd))
    return out.reshape(depth, b, nmod, d)


def _softplus(z):
    return jnp.maximum(z, 0.0) + jnp.log1p(jnp.exp(-jnp.abs(z)))


def _rwkv_proj_kernel(x_ref, xp_ref, mod_ref, mu_ref, vec_ref, wrkv_ref, w1_ref, w2_ref, a1_ref, a2_ref,
                      g1_ref, g2_ref, r_ref, lw_ref, k_ref, v_ref, kk_ref, a_ref, g_ref):
    s = pl.program_id(1)
    m = mod_ref[0]
    shift, scale = m[0:1], m[1:2]
    h = x_ref[0] * (1.0 + scale) + shift
    prev = xp_ref[0][7:8] * (1.0 + scale) + shift
    prev = jnp.where(s == 0, 0.0, prev)
    row = lax.broadcasted_iota(jnp.int32, h.shape, 0)
    hprev = jnp.where(row == 0, prev, pltpu.roll(h, 1, 0))
    xx = hprev - h
    mu = mu_ref[...]

    def mix(n):
        return (h + xx * mu[n:n + 1]).astype(BF16)

    vec = vec_ref[...]
    w0, a0, k_k, k_a = vec[0:1], vec[1:2], vec[2:3], vec[3:4]
    r = _dot(mix(0), wrkv_ref[0])
    k = _dot(mix(1), wrkv_ref[1])
    v = _dot(mix(2), wrkv_ref[2])
    wl = w0 + _dot(jnp.tanh(_dot(mix(3), w1_ref[...])).astype(BF16), w2_ref[...])
    w = -_softplus(-wl) - 0.5
    a = jax.nn.sigmoid(a0 + _dot(_dot(mix(4), a1_ref[...]).astype(BF16), a2_ref[...]))
    g = _dot(jax.nn.sigmoid(_dot(mix(5), g1_ref[...])).astype(BF16), g2_ref[...])
    r_ref[0] = r
    lw_ref[0] = -jnp.exp(w)
    k_ref[0] = k * (1.0 + (a - 1.0) * k_a)
    v_ref[0] = v
    kk_ref[0] = k * k_k
    a_ref[0] = a
    g_ref[0] = g.astype(BF16)


def _pad_cols(w, n):
    return jnp.pad(w, ((0, 0), (0, n - w.shape[1])))


def _pad_rows(w, n):
    return jnp.pad(w, ((0, n - w.shape[0]), (0, 0)))


def _rwkv_proj(x, mod_l, mu, w_rkv, w0, w1, w2, a0, a1, a2, g1, g2, k_k, k_a, ts):
    b, s, d = x.shape
    lora = 128
    glora = 256
    vec = jnp.zeros((8, d), F32).at[0].set(w0).at[1].set(a0).at[2].set(k_k).at[3].set(k_a)
    tile = pl.BlockSpec((1, ts, d), lambda i, j: (i, j, 0))
    full2 = lambda shape: pl.BlockSpec(shape, lambda i, j: (0, 0))
    outs = pl.pallas_call(
        _rwkv_proj_kernel,
        name="rwkv_proj",
        grid=(b, s // ts),
        in_specs=[
            tile,
            pl.BlockSpec((1, 8, d), lambda i, j: (i, jnp.maximum(j * (ts // 8) - 1, 0), 0)),
            pl.BlockSpec((1, 6, d), lambda i, j: (i, 0, 0)),
            full2((6, d)),
            full2((8, d)),
            pl.BlockSpec((3, d, d), lambda i, j: (0, 0, 0)),
            full2((d, lora)), full2((lora, d)),
            full2((d, lora)), full2((lora, d)),
            full2((d, glora)), full2((glora, d)),
        ],
        out_specs=[tile] * 7,
        out_shape=[jax.ShapeDtypeStruct((b, s, d), F32)] * 6 + [jax.ShapeDtypeStruct((b, s, d), BF16)],
        compiler_params=_params(("arbitrary", "arbitrary")),
    )(x, x, mod_l, mu, vec, w_rkv.astype(BF16),
      _pad_cols(w1, lora).astype(BF16), _pad_rows(w2, lora).astype(BF16),
      _pad_cols(a1, lora).astype(BF16), _pad_rows(a2, lora).astype(BF16),
      _pad_cols(g1, glora).astype(BF16), _pad_rows(g2, glora).astype(BF16))
    return outs


def _head_masks(shape):
    lane = lax.broadcasted_iota(jnp.int32, shape, len(shape) - 1)
    first = (lane % PAIR) < HEAD
    return first, jnp.logical_not(first)


def _stack_heads(z):
    m0, m1 = _head_masks(z.shape)
    return jnp.concatenate([jnp.where(m0, z, 0.0), jnp.where(m1, z, 0.0)], axis=0)


def _wkv_kernel(r_ref, lw_ref, k_ref, v_ref, kk_ref, a_ref, rk_ref, lng_ref, lnb_ref, y_ref,
                s_ref, yb_ref, *, nchunk):
    L = CHUNK

    @pl.when(pl.program_id(2) == 0)
    def _():
        s_ref[...] = jnp.zeros_like(s_ref)

    r2 = lax.broadcasted_iota(jnp.int32, (PAIR, PAIR), 0)
    c2 = lax.broadcasted_iota(jnp.int32, (PAIR, PAIR), 1)
    same_head = (r2 // HEAD) == (c2 // HEAD)
    eye = r2 == c2
    bd_ones = jnp.where(same_head, 1.0, 0.0).astype(F32)
    tr = lax.broadcasted_iota(jnp.int32, (L, PAIR), 0)
    tc = lax.broadcasted_iota(jnp.int32, (L, PAIR), 1) % HEAD
    strict = tr > tc
    incl = tr >= tc

    def off_diag(size):
        same = (tr // (2 * size)) == (tc // (2 * size))
        return same & ((tr % (2 * size)) >= size) & ((tc % (2 * size)) < size)
    lr = lax.broadcasted_iota(jnp.int32, (L, L), 0)
    lc = lax.broadcasted_iota(jnp.int32, (L, L), 1)
    ltri = jnp.where(lr >= lc, 1.0, 0.0).astype(F32)

    chunks = range(nchunk)
    r_all = r_ref[0]
    lw_all = lw_ref[0]
    k_all = k_ref[0]
    v_all = v_ref[0]
    a_all = a_ref[0]
    kkr = kk_ref[0]
    kk = kkr * lax.rsqrt(jnp.maximum(_dot_sel(kkr * kkr, bd_ones), 1e-24))
    bv_all = kk * a_all

    def rows(x, c):
        return x[c * L:(c + 1) * L]

    cs = [_sel_dot(ltri, rows(lw_all, c)) for c in chunks]
    rt, at, vst, lhs, rhs, bc, kc, p_last = [], [], [], [], [], [], [], []
    for c in chunks:
        cs_last = cs[c][L - 1:L, :]
        pinv = jnp.exp(-cs[c])
        prem = jnp.exp(cs_last - cs[c])
        rt.append(rows(r_all, c) * jnp.exp(cs[c]))
        at.append(-rows(kk, c) * jnp.exp(cs[c] - rows(lw_all, c)))
        bt = rows(bv_all, c) * pinv
        kt = rows(k_all, c) * pinv
        bc.append((rows(bv_all, c) * prem).astype(BF16))
        kc.append((rows(k_all, c) * prem).astype(BF16))
        p_last.append(jnp.exp(cs_last))
        vst.append(_stack_heads(rows(v_all, c)).astype(BF16))
        lhs.append(jnp.concatenate([at[c], rt[c]], axis=0).astype(BF16))
        rhs.append(jnp.concatenate([_stack_heads(bt), _stack_heads(kt)], axis=0).astype(BF16))
    o = [_dot_nt(lhs[c], rhs[c]) for c in chunks]
    nmat = [jnp.where(strict, o[c][:L, :PAIR], 0.0) for c in chunks]
    akv = [_dot(jnp.where(strict, o[c][:L, PAIR:], 0.0).astype(BF16), vst[c]) for c in chunks]
    ident = jnp.where(tr == tc, 1.0, 0.0)
    tinv = [ident + jnp.where(off_diag(1), nmat[c], 0.0) for c in chunks]
    size = 2
    while size < L:
        mask = off_diag(size)
        tn = [_dot(tinv[c].astype(BF16), _stack_heads(jnp.where(mask, nmat[c], 0.0)).astype(BF16)) for c in chunks]
        tinv = [tinv[c] + _dot(tn[c].astype(BF16), _stack_heads(tinv[c]).astype(BF16)) for c in chunks]
        size *= 2
    z = [_dot(tinv[c].astype(BF16), _stack_heads(jnp.concatenate([at[c], akv[c]], axis=1)).astype(BF16))
         for c in chunks]
    tmp = [_dot(jnp.where(incl, o[c][L:, :PAIR], 0.0).astype(BF16), _stack_heads(z[c]).astype(BF16))
           for c in chunks]
    rkv = [_dot(jnp.where(incl, o[c][L:, PAIR:], 0.0).astype(BF16), vst[c]) for c in chunks]
    bz = [_dot_tn(bc[c], z[c].astype(BF16)) for c in chunks]
    kv = [_dot_tn(kc[c], rows(v_all, c).astype(BF16)) for c in chunks]
    state = s_ref[...]
    for c in chunks:
        sb = state.astype(BF16)
        rh = (rt[c] + tmp[c][:, :PAIR]).astype(BF16)
        yb_ref[c * L:(c + 1) * L, :] = _dot(rh, sb) + tmp[c][:, PAIR:] + rkv[c]
        m_c = jnp.where(eye, jnp.broadcast_to(p_last[c], (PAIR, PAIR)), 0.0) + jnp.where(same_head, bz[c][:, :PAIR], 0.0)
        state = _dot(m_c.astype(BF16), sb) + jnp.where(same_head, bz[c][:, PAIR:] + kv[c], 0.0)
    s_ref[...] = state

    y = yb_ref[...]
    bd_avg = bd_ones * (1.0 / HEAD)
    ym = _dot_sel(y, bd_avg)
    d = y - ym
    yv = _dot_sel(d * d, bd_avg)
    yn = d * lax.rsqrt(yv + GN_EPS) * lng_ref[...] + lnb_ref[...]
    bonus = _dot_sel(r_all * k_all * rk_ref[...], bd_ones) * v_all
    y_ref[0] = yn + bonus


def _wkv_scan(r, lw, k, v, kkr, a, r_k, lnx_g, lnx_b, tc):
    b, s, d = r.shape
    nchunk = tc // CHUNK
    tile = pl.BlockSpec((1, tc, PAIR), lambda i, p, j: (i, j, p))
    row = pl.BlockSpec((1, PAIR), lambda i, p, j: (0, p))
    return pl.pallas_call(
        functools.partial(_wkv_kernel, nchunk=nchunk),
        name="wkv_scan",
        grid=(b, d // PAIR, s // tc),
        in_specs=[tile] * 6 + [row] * 3,
        out_specs=tile,
        out_shape=jax.ShapeDtypeStruct((b, s, d), F32),
        scratch_shapes=[
            pltpu.VMEM((PAIR, PAIR), F32),
            pltpu.VMEM((tc, PAIR), F32),
        ],
        compiler_params=_params(("arbitrary", "arbitrary", "arbitrary")),
    )(r, lw, k, v, kkr, a, r_k.reshape(1, d), lnx_g.reshape(1, d), lnx_b.reshape(1, d))


def _mix_out_kernel(y_ref, g_ref, x_ref, mod_ref, wo_ref, ln_ref, o_ref):
    m = mod_ref[0]
    gate = m[2:3]
    yg = (y_ref[0] * g_ref[0].astype(F32)).astype(BF16)
    o = _dot(yg, wo_ref[...])
    z = ALPHA * x_ref[0] + gate * o
    ln = ln_ref[...]
    o_ref[0] = _layer_norm_rows(z, ln[0:1], ln[1:2])


def _mix_out(y, g, x, mod_l, w_o, ln_g, ln_b, ts):
    b, s, d = x.shape
    tile = pl.BlockSpec((1, ts, d), lambda i, j: (i, j, 0))
    ln = jnp.zeros((8, d), F32).at[0].set(ln_g).at[1].set(ln_b)
    return pl.pallas_call(
        _mix_out_kernel,
        name="mix_out",
        grid=(b, s // ts),
        in_specs=[tile, tile, tile,
                  pl.BlockSpec((1, 6, d), lambda i, j: (i, 0, 0)),
                  pl.BlockSpec((d, d), lambda i, j: (0, 0)),
                  pl.BlockSpec((8, d), lambda i, j: (0, 0))],
        out_specs=tile,
        out_shape=jax.ShapeDtypeStruct((b, s, d), F32),
        compiler_params=_params(("arbitrary", "arbitrary")),
    )(y, g, x, mod_l, w_o.astype(BF16), ln)


def _pool_kernel(x_ref, mod_ref, win_ref, wgrp_ref, sc_ref, wout_ref, ln_ref, o_ref, zext_ref, *, ts):
    s = pl.program_id(1)
    m = mod_ref[0]
    shift, scale, gate = m[0:1], m[1:2], m[2:3]
    x = x_ref[0]
    h = x * (1.0 + scale) + shift
    z = _dot(h.astype(BF16), win_ref[...])

    @pl.when(s == 0)
    def _():
        zext_ref[0:POOL_HALO, :] = jnp.zeros((POOL_HALO, z.shape[1]), F32)

    zext_ref[POOL_HALO:POOL_HALO + ts, :] = z
    pos = s * ts + lax.broadcasted_iota(jnp.int32, (ts, 1), 0)
    gw = z.shape[1] // len(POOL_WINDOWS)
    parts = []
    for gi, win in enumerate(POOL_WINDOWS):
        lo = gi * gw
        zg = z[:, lo:lo + gw]
        acc = zg
        for back in range(1, win):
            acc = acc + zext_ref[POOL_HALO - back:POOL_HALO - back + ts, lo:lo + gw]
        cnt = jnp.minimum(pos + 1, win).astype(F32)
        p = acc / cnt - zg
        parts.append(_dot(p.astype(BF16), wgrp_ref[gi]))
    y = jnp.concatenate(parts, axis=1) * sc_ref[...]
    o = _dot(y.astype(BF16), wout_ref[...])
    zext_ref[0:POOL_HALO, :] = zext_ref[ts:ts + POOL_HALO, :]
    ln = ln_ref[...]
    o_ref[0] = _layer_norm_rows(ALPHA * x + gate * o, ln[0:1], ln[1:2])


def _pool_mix(x, mod_l, w_in, w_grp, scale, w_out, ln_g, ln_b, ts):
    b, s, d = x.shape
    ng, gw, _ = w_grp.shape
    tile = pl.BlockSpec((1, ts, d), lambda i, j: (i, j, 0))
    ln = jnp.zeros((8, d), F32).at[0].set(ln_g).at[1].set(ln_b)
    return pl.pallas_call(
        functools.partial(_pool_kernel, ts=ts),
        name="pool_mix",
        grid=(b, s // ts),
        in_specs=[tile,
                  pl.BlockSpec((1, 6, d), lambda i, j: (i, 0, 0)),
                  pl.BlockSpec((d, d), lambda i, j: (0, 0)),
                  pl.BlockSpec((ng, gw, gw), lambda i, j: (0, 0, 0)),
                  pl.BlockSpec((1, d), lambda i, j: (0, 0)),
                  pl.BlockSpec((d, d), lambda i, j: (0, 0)),
                  pl.BlockSpec((8, d), lambda i, j: (0, 0))],
        out_specs=tile,
        out_shape=jax.ShapeDtypeStruct((b, s, d), F32),
        scratch_shapes=[pltpu.VMEM((ts + POOL_HALO, d), F32)],
        compiler_params=_params(("arbitrary", "arbitrary")),
    )(x, mod_l, w_in.astype(BF16), w_grp.astype(BF16), scale.reshape(1, d), w_out.astype(BF16), ln)


def _fold_kernel(keys_ref, wq_ref, o_ref):
    o_ref[0] = lax.dot_general(keys_ref[0, 0], wq_ref[0], NT_DIMS, precision=HI,
                               preferred_element_type=F32).astype(BF16)


def _peer_fold(keys, w_q):
    depth, nh, two, nk, dk = keys.shape
    d = w_q.shape[1]
    nb = nh * two
    return pl.pallas_call(
        _fold_kernel,
        name="peer_fold",
        grid=(depth, nb),
        in_specs=[pl.BlockSpec((1, 1, nk, dk), lambda l, j: (l, j, 0, 0)),
                  pl.BlockSpec((1, d, dk), lambda l, j: (l, 0, j))],
        out_specs=pl.BlockSpec((1, nk, d), lambda l, j: (l, j, 0)),
        out_shape=jax.ShapeDtypeStruct((depth, nb * nk, d), BF16),
        compiler_params=_params(("arbitrary", "arbitrary")),
    )(keys.reshape(depth, nb, nk, dk), w_q)


N_TOP = PEER_TOPK + 1
CAND = [(a, b) for a in range(N_TOP) for b in range(N_TOP) if (a + 1) * (b + 1) <= N_TOP]
CAND_ROWS = -(-len(CAND) // SUB) * SUB
TOP_ROWS = -(-N_TOP // SUB) * SUB
KEY_GROUPS = N_KEYS // SUB


def _sorting_network(n):
    pairs = []

    def merge(lo, cnt, step):
        nxt = step * 2
        if nxt < cnt:
            merge(lo, cnt, nxt)
            merge(lo + step, cnt, nxt)
            for i in range(lo + step, lo + cnt - step, nxt):
                pairs.append((i, i + step))
        else:
            pairs.append((lo, lo + step))

    def sort(lo, cnt):
        if cnt > 1:
            half = cnt // 2
            sort(lo, half)
            sort(lo + half, half)
            merge(lo, cnt, 1)

    sort(0, n)
    return pairs


GROUP_SORT = _sorting_network(KEY_GROUPS)


def _top_values(cur, out_ref, n):
    row = lax.broadcasted_iota(jnp.int32, cur.shape, 0)

    def body(i, cur):
        mx = jnp.max(cur, axis=0, keepdims=True)
        out_ref[pl.ds(i, 1), :] = mx
        first = jnp.min(jnp.where(cur == mx, row, cur.shape[0]), axis=0, keepdims=True)
        return jnp.where(row == first, NEG_INF, cur)

    lax.fori_loop(0, n, body, cur)


def _top_sorted(s, n):
    cols = [s[g * SUB:(g + 1) * SUB, :] for g in range(KEY_GROUPS)]
    for i, j in GROUP_SORT:
        cols[i], cols[j] = jnp.maximum(cols[i], cols[j]), jnp.minimum(cols[i], cols[j])
    sub = lax.broadcasted_iota(jnp.int32, cols[0].shape, 0)
    outs = []
    for it in range(n):
        mx = jnp.max(cols[0], axis=0, keepdims=True)
        outs.append(mx)
        left = n - 1 - it
        if left == 0:
            break
        first = jnp.min(jnp.where(cols[0] == mx, sub, SUB), axis=0, keepdims=True)
        pop = sub == first
        for lvl in range(min(KEY_GROUPS, left)):
            below = cols[lvl + 1] if lvl + 1 < KEY_GROUPS else NEG_INF
            cols[lvl] = jnp.where(pop, below, cols[lvl])
    return outs


def _route_kernel(x_ref, mod_ref, wf_ref, h2_ref, e2_ref, rk_ref, c1_ref, n1_ref, tc_ref):
    hd = pl.program_id(1)
    m = mod_ref[0]
    shift, scale = m[3:4], m[4:5]

    @pl.when(hd == 0)
    def _():
        h2_ref[...] = (x_ref[...] * (1.0 + scale) + shift).astype(BF16)

    h2 = h2_ref[...]
    s1 = _dot_nt(wf_ref[0:N_KEYS, :], h2)
    s2 = _dot_nt(wf_ref[N_KEYS:2 * N_KEYS, :], h2)
    ta = _top_sorted(s1, N_TOP)
    tb = _top_sorted(s2, N_TOP)
    rows = [ta[a] + tb[b] for a, b in CAND]
    rows += [jnp.full_like(rows[0], NEG_INF)] * (CAND_ROWS - len(CAND))
    cand = jnp.concatenate(rows, axis=0)
    _top_values(cand, tc_ref, N_TOP)
    tcv = tc_ref[...]
    tau = 0.5 * (tcv[PEER_TOPK - 1:PEER_TOPK] + tcv[PEER_TOPK:PEER_TOPK + 1])
    top = ta[0] + tb[0]
    zsum = jnp.sum(jnp.where(cand >= tau, jnp.exp(cand - top), 0.0), axis=0, keepdims=True)
    need = tau - s1
    rank2 = jnp.zeros_like(s2)
    count1 = jnp.zeros_like(s1)
    for b in range(PEER_TOPK):
        rank2 = rank2 + jnp.where(tb[b] > s2, 1.0, 0.0)
        count1 = count1 + jnp.where(tb[b] >= need, 1.0, 0.0)
    e2_ref[0] = jnp.exp(s2 - tb[0]).astype(BF16)
    rk_ref[0] = rank2.astype(BF16)
    c1_ref[0] = jnp.exp(s1 - ta[0]) / zsum
    n1_ref[0] = count1


def _peer_route(x2d, mod_l, wf, tt, tiles_per_batch):
    t, d = x2d.shape
    nh = PEER_HEADS
    per_head = pl.BlockSpec((1, N_KEYS, tt), lambda i, h: (h, 0, i))
    return pl.pallas_call(
        _route_kernel,
        name="peer_route",
        grid=(t // tt, nh),
        in_specs=[pl.BlockSpec((tt, d), lambda i, h: (i, 0)),
                  pl.BlockSpec((1, 6, d), lambda i, h: (i // tiles_per_batch, 0, 0)),
                  pl.BlockSpec((2 * N_KEYS, d), lambda i, h: (h, 0))],
        out_specs=[pl.BlockSpec((tt, d), lambda i, h: (i, 0)), per_head, per_head, per_head, per_head],
        out_shape=[jax.ShapeDtypeStruct((t, d), BF16),
                   jax.ShapeDtypeStruct((nh, N_KEYS, t), BF16),
                   jax.ShapeDtypeStruct((nh, N_KEYS, t), BF16),
                   jax.ShapeDtypeStruct((nh, N_KEYS, t), F32),
                   jax.ShapeDtypeStruct((nh, N_KEYS, t), F32)],
        scratch_shapes=[pltpu.VMEM((TOP_ROWS, tt), F32)],
        compiler_params=_params(("arbitrary", "arbitrary")),
    )(x2d, mod_l, wf)


ROWS = 16
TOKEN_COLS = 256
MXU_ROWS = 256
MXU_COLS = 256


def _dense_kernel(h2_ref, e2_ref, rk_ref, c1_ref, n1_ref, u_ref, vt_ref, x_ref, mod_ref, ln_ref, o_ref,
                  acc_ref, za_ref, zb_ref, pa_ref, pb_ref, *, eb):
    e = pl.program_id(1)
    last = pl.num_programs(1) - 1
    tt = h2_ref.shape[0]
    slabs = eb // N_KEYS

    @pl.when(e == 0)
    def _():
        acc_ref[...] = jnp.zeros_like(acc_ref)
        zb_ref[...] = jnp.zeros_like(zb_ref)
        pa_ref[...] = jnp.zeros_like(pa_ref)

    def gate_blocks(z_ref, p_ref):
        blocks = []
        for il in range(slabs):
            i1 = jnp.clip((e - 1) * slabs + il, 0, N_KEYS - 1)
            for t0 in range(0, tt, TOKEN_COLS):
                cols = slice(t0, t0 + TOKEN_COLS)
                rows_of = {}

                def token_rows(i1=i1, cols=cols, rows_of=rows_of):
                    if not rows_of:
                        rows_of["cnt"] = [jnp.broadcast_to(n1_ref[hd, pl.ds(i1, 1), cols], (ROWS, TOKEN_COLS)).astype(BF16)
                                          for hd in range(PEER_HEADS)]
                        rows_of["wgt"] = [jnp.broadcast_to(c1_ref[hd, pl.ds(i1, 1), cols], (ROWS, TOKEN_COLS)).astype(BF16)
                                          for hd in range(PEER_HEADS)]
                    return rows_of["cnt"], rows_of["wgt"]

                for jb in range(N_KEYS // ROWS):
                    def block(il=il, jb=jb, cols=cols, token_rows=token_rows):
                        cnt, wgt = token_rows()
                        r0 = il * N_KEYS + jb * ROWS
                        zz = z_ref[r0:r0 + ROWS, cols]
                        act = (0.5 * zz * (1.0 + lax.erf(zz * INV_SQRT2))).astype(BF16)
                        gsum = jnp.zeros((ROWS, TOKEN_COLS), BF16)
                        for hd in range(PEER_HEADS):
                            keep = rk_ref[hd, jb * ROWS:(jb + 1) * ROWS, cols] < cnt[hd]
                            gsum = gsum + jnp.where(keep, e2_ref[hd, jb * ROWS:(jb + 1) * ROWS, cols], 0.0) * wgt[hd]
                        p_ref[r0:r0 + ROWS, cols] = act * gsum
                    blocks.append(block)
        return blocks

    def matmul_units(z_new, p_old):
        units = []
        for n0 in range(0, tt, MXU_COLS):
            for m0 in range(0, acc_ref.shape[0], MXU_ROWS):
                def unit(m0=m0, n0=n0):
                    acc_ref[m0:m0 + MXU_ROWS, n0:n0 + MXU_COLS] += _dot(vt_ref[0, m0:m0 + MXU_ROWS, :], p_old[:, n0:n0 + MXU_COLS])
                units.append((eb, unit))
            for m0 in range(0, eb, MXU_ROWS):
                def unit(m0=m0, n0=n0):
                    z_new[m0:m0 + MXU_ROWS, n0:n0 + MXU_COLS] = _dot_nt(u_ref[m0:m0 + MXU_ROWS, :], h2_ref[n0:n0 + MXU_COLS, :])
                units.append((u_ref.shape[1], unit))
        return units

    def step(z_new, z_old, p_new, p_old):
        blocks = gate_blocks(z_old, p_new)
        units = matmul_units(z_new, p_old)
        total = sum(k for k, _ in units)
        done = 0
        issued = 0
        for k, unit in units:
            unit()
            done += k
            upto = -(-len(blocks) * done // total)
            for block in blocks[issued:upto]:
                block()
            issued = upto

    @pl.when(e % 2 == 0)
    def _():
        step(za_ref, zb_ref, pb_ref, pa_ref)

    @pl.when(e % 2 == 1)
    def _():
        step(zb_ref, za_ref, pa_ref, pb_ref)

    @pl.when(e == last)
    def _():
        m = mod_ref[0]
        gate = m[5:6]
        ln = ln_ref[...]
        y = acc_ref[...].T
        o_ref[...] = _layer_norm_rows(ALPHA * x_ref[...] + gate * y, ln[0:1], ln[1:2])


def _peer_dense(h2, e2, rk2, c1, n1, u_bf, vt_bf, x2d, mod_l, ln_g, ln_b, tt, eb, tiles_per_batch):
    t, d = x2d.shape
    ne = u_bf.shape[0]
    nh = PEER_HEADS
    ln = jnp.zeros((8, d), F32).at[0].set(ln_g).at[1].set(ln_b)
    per_tile = pl.BlockSpec((nh, N_KEYS, tt), lambda i, e: (0, 0, i))
    nslab = ne // eb
    return pl.pallas_call(
        functools.partial(_dense_kernel, eb=eb),
        name="peer_dense",
        grid=(t // tt, nslab + 2),
        in_specs=[pl.BlockSpec((tt, d), lambda i, e: (i, 0)),
                  per_tile, per_tile, per_tile, per_tile,
                  pl.BlockSpec((eb, d), lambda i, e: (jnp.minimum(e, nslab - 1), 0)),
                  pl.BlockSpec((1, d, eb), lambda i, e: (jnp.clip(e - 2, 0, nslab - 1), 0, 0)),
                  pl.BlockSpec((tt, d), lambda i, e: (i, 0)),
                  pl.BlockSpec((1, 6, d), lambda i, e: (i // tiles_per_batch, 0, 0)),
                  pl.BlockSpec((8, d), lambda i, e: (0, 0))],
        out_specs=pl.BlockSpec((tt, d), lambda i, e: (i, 0)),
        out_shape=jax.ShapeDtypeStruct((t, d), F32),
        scratch_shapes=[pltpu.VMEM((d, tt), F32),
                        pltpu.VMEM((eb, tt), F32),
                        pltpu.VMEM((eb, tt), F32),
                        pltpu.VMEM((eb, tt), BF16),
                        pltpu.VMEM((eb, tt), BF16)],
        compiler_params=_params(("arbitrary", "arbitrary")),
    )(h2, e2, rk2, c1, n1, u_bf, vt_bf, x2d, mod_l, ln)


def _peer_ffn(x, mod_l, wf, u_tab, v_tab, ln_g, ln_b, tt, eb):
    b, s, d = x.shape
    x2d = x.reshape(b * s, d)
    tpb = s // tt
    h2, e2, rk2, c1, n1 = _peer_route(x2d, mod_l, wf, tt, tpb)
    vt = v_tab.astype(BF16).reshape(v_tab.shape[0] // eb, eb, d).transpose(0, 2, 1)
    out = _peer_dense(h2, e2, rk2, c1, n1, u_tab.astype(BF16), vt, x2d, mod_l, ln_g, ln_b, tt, eb, tpb)
    return out.reshape(b, s, d)


def kernel(x, c, ada_w, ada_b, ln_g, ln_b, rw_mu, rw_w_rkv, rw_w0, rw_w1, rw_w2, rw_a0, rw_a1, rw_a2, rw_g1, rw_g2, rw_k_k, rw_k_a, rw_r_k, rw_lnx_g, rw_lnx_b, rw_w_o, pl_w_in, pl_w_grp, pl_scale, pl_w_out, pe_w_q, pe_keys, pe_u, pe_v):
    b, s, d = x.shape
    depth = ada_w.shape[0]
    ts = min(256, s)
    tc = min(512, s)
    tt = min(512, s)
    eb = 512
    mod = _adaln_mod(c, ada_w, ada_b)
    wf = _peer_fold(pe_keys, pe_w_q)
    for i in range(depth):
        j = i // 2
        if i % 2 == 0:
            r, lw, k, v, kkr, a, g = _rwkv_proj(x, mod[i], rw_mu[j], rw_w_rkv[j], rw_w0[j], rw_w1[j], rw_w2[j],
                                                rw_a0[j], rw_a1[j], rw_a2[j], rw_g1[j], rw_g2[j],
                                                rw_k_k[j], rw_k_a[j], ts)
            y = _wkv_scan(r, lw, k, v, kkr, a, rw_r_k[j], rw_lnx_g[j], rw_lnx_b[j], tc)
            x = _mix_out(y, g, x, mod[i], rw_w_o[j], ln_g[i, 0], ln_b[i, 0], ts)
        else:
            x = _pool_mix(x, mod[i], pl_w_in[j], pl_w_grp[j], pl_scale[j], pl_w_out[j],
                          ln_g[i, 0], ln_b[i, 0], ts)
        x = _peer_ffn(x, mod[i], wf[i], pe_u[i], pe_v[i], ln_g[i, 1], ln_b[i, 1], tt, eb)
    return x
```

```python
import functools
import math

import jax
import jax.numpy as jnp
from jax import lax
from jax.experimental import pallas as pl
from jax.experimental.pallas import tpu as pltpu

F32 = jnp.float32
BF16 = jnp.bfloat16
HI = lax.Precision.HIGHEST

HEAD = 64
PAIR = 2 * HEAD
SUB = 8
CHUNK = 64
GN_EPS = 64e-5
LN_EPS = 1e-5
DEPTH = 2
ALPHA = (2 * DEPTH) ** 0.25
POOL_WINDOWS = (2, 4, 8, 16)
POOL_HALO = 16
N_KEYS = 128
PEER_HEADS = 8
PEER_TOPK = 16
INV_SQRT2 = 0.7071067811865476
NEG_INF = float("-inf")
V7X_VMEM_LIMIT = 56 * 1024 * 1024

NT_DIMS = (((1,), (1,)), ((), ()))
TN_DIMS = (((0,), (0,)), ((), ()))


def _dot(a, b, precision=None):
    return jnp.dot(a, b, precision=precision, preferred_element_type=F32)


def _dot_nt(a, b):
    return lax.dot_general(a, b, NT_DIMS, preferred_element_type=F32)


def _dot_tn(a, b):
    return lax.dot_general(a, b, TN_DIMS, preferred_element_type=F32)


def _split_bf16(x):
    hi = x.astype(BF16)
    return hi, (x - hi.astype(F32)).astype(BF16)


def _dot_sel(x, sel):
    hi, lo = _split_bf16(x)
    sel = sel.astype(BF16)
    return _dot(hi, sel) + _dot(lo, sel)


def _sel_dot(sel, x):
    hi, lo = _split_bf16(x)
    sel = sel.astype(BF16)
    return _dot(sel, hi) + _dot(sel, lo)


def _params(sem, vmem=V7X_VMEM_LIMIT, flags=None):
    return pltpu.CompilerParams(dimension_semantics=sem, vmem_limit_bytes=vmem, flags=flags)


def _layer_norm_rows(z, g, b):
    mu = jnp.mean(z, axis=-1, keepdims=True)
    d = z - mu
    var = jnp.mean(d * d, axis=-1, keepdims=True)
    return d * lax.rsqrt(var + LN_EPS) * g + b


def _mod_kernel(c_ref, w_ref, b_ref, o_ref):
    c = c_ref[...]
    cond = c * jax.nn.sigmoid(c)
    o_ref[0] = _dot(cond, w_ref[0], HI) + b_ref[0]


def _adaln_mod(c, ada_w, ada_b):
    depth, d, nd = ada_w.shape
    b = c.shape[0]
    nmod = nd // d
    out = pl.pallas_call(
        _mod_kernel,
        name="adaln_mod",
        grid=(depth, nmod),
        in_specs=[
            pl.BlockSpec((b, d), lambda l, n: (0, 0)),
            pl.BlockSpec((1, d, d), lambda l, n: (l, 0, n)),
            pl.BlockSpec((1, 1, d), lambda l, n: (l, 0, n)),
        ],
        out_specs=pl.BlockSpec((1, b, d), lambda l, n: (l, 0, n)),
        out_shape=jax.ShapeDtypeStruct((depth, b, nd), F32),
        compiler_params=_params(("arbitrary", "arbitrary")),
    )(c, ada_w, ada_b.reshape(depth, 1, nd))
    return out.reshape(depth, b, nmod, d)


def _softplus(z):
    return jnp.maximum(z, 0.0) + jnp.log1p(jnp.exp(-jnp.abs(z)))


def _rwkv_proj_kernel(x_ref, xp_ref, mod_ref, mu_ref, vec_ref, wrkv_ref, w1_ref, w2_ref, a1_ref, a2_ref,
                      g1_ref, g2_ref, r_ref, lw_ref, k_ref, v_ref, kk_ref, a_ref, g_ref):
    s = pl.program_id(1)
    m = mod_ref[0]
    shift, scale = m[0:1], m[1:2]
    h = x_ref[0] * (1.0 + scale) + shift
    prev = xp_ref[0][7:8] * (1.0 + scale) + shift
    prev = jnp.where(s == 0, 0.0, prev)
    row = lax.broadcasted_iota(jnp.int32, h.shape, 0)
    hprev = jnp.where(row == 0, prev, pltpu.roll(h, 1, 0))
    xx = hprev - h
    mu = mu_ref[...]

    def mix(n):
        return (h + xx * mu[n:n + 1]).astype(BF16)

    vec = vec_ref[...]
    w0, a0, k_k, k_a = vec[0:1], vec[1:2], vec[2:3], vec[3:4]
    r = _dot(mix(0), wrkv_ref[0])
    k = _dot(mix(1), wrkv_ref[1])
    v = _dot(mix(2), wrkv_ref[2])
    wl = w0 + _dot(jnp.tanh(_dot(mix(3), w1_ref[...])).astype(BF16), w2_ref[...])
    w = -_softplus(-wl) - 0.5
    a = jax.nn.sigmoid(a0 + _dot(_dot(mix(4), a1_ref[...]).astype(BF16), a2_ref[...]))
    g = _dot(jax.nn.sigmoid(_dot(mix(5), g1_ref[...])).astype(BF16), g2_ref[...])
    r_ref[0] = r
    lw_ref[0] = -jnp.exp(w)
    k_ref[0] = k * (1.0 + (a - 1.0) * k_a)
    v_ref[0] = v
    kk_ref[0] = k * k_k
    a_ref[0] = a
    g_ref[0] = g.astype(BF16)


def _pad_cols(w, n):
    return jnp.pad(w, ((0, 0), (0, n - w.shape[1])))


def _pad_rows(w, n):
    return jnp.pad(w, ((0, n - w.shape[0]), (0, 0)))


def _rwkv_proj(x, mod_l, mu, w_rkv, w0, w1, w2, a0, a1, a2, g1, g2, k_k, k_a, ts):
    b, s, d = x.shape
    lora = 128
    glora = 256
    vec = jnp.zeros((8, d), F32).at[0].set(w0).at[1].set(a0).at[2].set(k_k).at[3].set(k_a)
    tile = pl.BlockSpec((1, ts, d), lambda i, j: (i, j, 0))
    full2 = lambda shape: pl.BlockSpec(shape, lambda i, j: (0, 0))
    outs = pl.pallas_call(
        _rwkv_proj_kernel,
        name="rwkv_proj",
        grid=(b, s // ts),
        in_specs=[
            tile,
            pl.BlockSpec((1, 8, d), lambda i, j: (i, jnp.maximum(j * (ts // 8) - 1, 0), 0)),
            pl.BlockSpec((1, 6, d), lambda i, j: (i, 0, 0)),
            full2((6, d)),
            full2((8, d)),
            pl.BlockSpec((3, d, d), lambda i, j: (0, 0, 0)),
            full2((d, lora)), full2((lora, d)),
            full2((d, lora)), full2((lora, d)),
            full2((d, glora)), full2((glora, d)),
        ],
        out_specs=[tile] * 7,
        out_shape=[jax.ShapeDtypeStruct((b, s, d), F32)] * 6 + [jax.ShapeDtypeStruct((b, s, d), BF16)],
        compiler_params=_params(("arbitrary", "arbitrary")),
    )(x, x, mod_l, mu, vec, w_rkv.astype(BF16),
      _pad_cols(w1, lora).astype(BF16), _pad_rows(w2, lora).astype(BF16),
      _pad_cols(a1, lora).astype(BF16), _pad_rows(a2, lora).astype(BF16),
      _pad_cols(g1, glora).astype(BF16), _pad_rows(g2, glora).astype(BF16))
    return outs


def _head_masks(shape):
    lane = lax.broadcasted_iota(jnp.int32, shape, len(shape) - 1)
    first = (lane % PAIR) < HEAD
    return first, jnp.logical_not(first)


def _stack_heads(z):
    m0, m1 = _head_masks(z.shape)
    return jnp.concatenate([jnp.where(m0, z, 0.0), jnp.where(m1, z, 0.0)], axis=0)


def _wkv_kernel(r_ref, lw_ref, k_ref, v_ref, kk_ref, a_ref, rk_ref, lng_ref, lnb_ref, y_ref,
                s_ref, yb_ref, *, nchunk):
    L = CHUNK

    @pl.when(pl.program_id(2) == 0)
    def _():
        s_ref[...] = jnp.zeros_like(s_ref)

    r2 = lax.broadcasted_iota(jnp.int32, (PAIR, PAIR), 0)
    c2 = lax.broadcasted_iota(jnp.int32, (PAIR, PAIR), 1)
    same_head = (r2 // HEAD) == (c2 // HEAD)
    eye = r2 == c2
    bd_ones = jnp.where(same_head, 1.0, 0.0).astype(F32)
    tr = lax.broadcasted_iota(jnp.int32, (L, PAIR), 0)
    tc = lax.broadcasted_iota(jnp.int32, (L, PAIR), 1) % HEAD
    strict = tr > tc
    incl = tr >= tc

    def off_diag(size):
        same = (tr // (2 * size)) == (tc // (2 * size))
        return same & ((tr % (2 * size)) >= size) & ((tc % (2 * size)) < size)
    lr = lax.broadcasted_iota(jnp.int32, (L, L), 0)
    lc = lax.broadcasted_iota(jnp.int32, (L, L), 1)
    ltri = jnp.where(lr >= lc, 1.0, 0.0).astype(F32)

    chunks = range(nchunk)
    r_all = r_ref[0]
    lw_all = lw_ref[0]
    k_all = k_ref[0]
    v_all = v_ref[0]
    a_all = a_ref[0]
    kkr = kk_ref[0]
    kk = kkr * lax.rsqrt(jnp.maximum(_dot_sel(kkr * kkr, bd_ones), 1e-24))
    bv_all = kk * a_all

    def rows(x, c):
        return x[c * L:(c + 1) * L]

    cs = [_sel_dot(ltri, rows(lw_all, c)) for c in chunks]
    rt, at, vst, lhs, rhs, bc, kc, p_last = [], [], [], [], [], [], [], []
    for c in chunks:
        cs_last = cs[c][L - 1:L, :]
        pinv = jnp.exp(-cs[c])
        prem = jnp.exp(cs_last - cs[c])
        rt.append(rows(r_all, c) * jnp.exp(cs[c]))
        at.append(-rows(kk, c) * jnp.exp(cs[c] - rows(lw_all, c)))
        bt = rows(bv_all, c) * pinv
        kt = rows(k_all, c) * pinv
        bc.append((rows(bv_all, c) * prem).astype(BF16))
        kc.append((rows(k_all, c) * prem).astype(BF16))
        p_last.append(jnp.exp(cs_last))
        vst.append(_stack_heads(rows(v_all, c)).astype(BF16))
        lhs.append(jnp.concatenate([at[c], rt[c]], axis=0).astype(BF16))
        rhs.append(jnp.concatenate([_stack_heads(bt), _stack_heads(kt)], axis=0).astype(BF16))
    o = [_dot_nt(lhs[c], rhs[c]) for c in chunks]
    nmat = [jnp.where(strict, o[c][:L, :PAIR], 0.0) for c in chunks]
    akv = [_dot(jnp.where(strict, o[c][:L, PAIR:], 0.0).astype(BF16), vst[c]) for c in chunks]
    ident = jnp.where(tr == tc, 1.0, 0.0)
    tinv = [ident + jnp.where(off_diag(1), nmat[c], 0.0) for c in chunks]
    size = 2
    while size < L:
        mask = off_diag(size)
        tn = [_dot(tinv[c].astype(BF16), _stack_heads(jnp.where(mask, nmat[c], 0.0)).astype(BF16)) for c in chunks]
        tinv = [tinv[c] + _dot(tn[c].astype(BF16), _stack_heads(tinv[c]).astype(BF16)) for c in chunks]
        size *= 2
    z = [_dot(tinv[c].astype(BF16), _stack_heads(jnp.concatenate([at[c], akv[c]], axis=1)).astype(BF16))
         for c in chunks]
    tmp = [_dot(jnp.where(incl, o[c][L:, :PAIR], 0.0).astype(BF16), _stack_heads(z[c]).astype(BF16))
           for c in chunks]
    rkv = [_dot(jnp.where(incl, o[c][L:, PAIR:], 0.0).astype(BF16), vst[c]) for c in chunks]
    bz = [_dot_tn(bc[c], z[c].astype(BF16)) for c in chunks]
    kv = [_dot_tn(kc[c], rows(v_all, c).astype(BF16)) for c in chunks]
    state = s_ref[...]
    for c in chunks:
        sb = state.astype(BF16)
        rh = (rt[c] + tmp[c][:, :PAIR]).astype(BF16)
        yb_ref[c * L:(c + 1) * L, :] = _dot(rh, sb) + tmp[c][:, PAIR:] + rkv[c]
        m_c = jnp.where(eye, jnp.broadcast_to(p_last[c], (PAIR, PAIR)), 0.0) + jnp.where(same_head, bz[c][:, :PAIR], 0.0)
        state = _dot(m_c.astype(BF16), sb) + jnp.where(same_head, bz[c][:, PAIR:] + kv[c], 0.0)
    s_ref[...] = state

    y = yb_ref[...]
    bd_avg = bd_ones * (1.0 / HEAD)
    ym = _dot_sel(y, bd_avg)
    d = y - ym
    yv = _dot_sel(d * d, bd_avg)
    yn = d * lax.rsqrt(yv + GN_EPS) * lng_ref[...] + lnb_ref[...]
    bonus = _dot_sel(r_all * k_all * rk_ref[...], bd_ones) * v_all
    y_ref[0] = yn + bonus


def _wkv_scan(r, lw, k, v, kkr, a, r_k, lnx_g, lnx_b, tc):
    b, s, d = r.shape
    nchunk = tc // CHUNK
    tile = pl.BlockSpec((1, tc, PAIR), lambda i, p, j: (i, j, p))
    row = pl.BlockSpec((1, PAIR), lambda i, p, j: (0, p))
    return pl.pallas_call(
        functools.partial(_wkv_kernel, nchunk=nchunk),
        name="wkv_scan",
        grid=(b, d // PAIR, s // tc),
        in_specs=[tile] * 6 + [row] * 3,
        out_specs=tile,
        out_shape=jax.ShapeDtypeStruct((b, s, d), F32),
        scratch_shapes=[
            pltpu.VMEM((PAIR, PAIR), F32),
            pltpu.VMEM((tc, PAIR), F32),
        ],
        compiler_params=_params(("arbitrary", "arbitrary", "arbitrary")),
    )(r, lw, k, v, kkr, a, r_k.reshape(1, d), lnx_g.reshape(1, d), lnx_b.reshape(1, d))


def _mix_out_kernel(y_ref, g_ref, x_ref, mod_ref, wo_ref, ln_ref, o_ref):
    m = mod_ref[0]
    gate = m[2:3]
    yg = (y_ref[0] * g_ref[0].astype(F32)).astype(BF16)
    o = _dot(yg, wo_ref[...])
    z = ALPHA * x_ref[0] + gate * o
    ln = ln_ref[...]
    o_ref[0] = _layer_norm_rows(z, ln[0:1], ln[1:2])


def _mix_out(y, g, x, mod_l, w_o, ln_g, ln_b, ts):
    b, s, d = x.shape
    tile = pl.BlockSpec((1, ts, d), lambda i, j: (i, j, 0))
    ln = jnp.zeros((8, d), F32).at[0].set(ln_g).at[1].set(ln_b)
    return pl.pallas_call(
        _mix_out_kernel,
        name="mix_out",
        grid=(b, s // ts),
        in_specs=[tile, tile, tile,
                  pl.BlockSpec((1, 6, d), lambda i, j: (i, 0, 0)),
                  pl.BlockSpec((d, d), lambda i, j: (0, 0)),
                  pl.BlockSpec((8, d), lambda i, j: (0, 0))],
        out_specs=tile,
        out_shape=jax.ShapeDtypeStruct((b, s, d), F32),
        compiler_params=_params(("arbitrary", "arbitrary")),
    )(y, g, x, mod_l, w_o.astype(BF16), ln)


def _pool_kernel(x_ref, mod_ref, win_ref, wgrp_ref, sc_ref, wout_ref, ln_ref, o_ref, zext_ref, *, ts):
    s = pl.program_id(1)
    m = mod_ref[0]
    shift, scale, gate = m[0:1], m[1:2], m[2:3]
    x = x_ref[0]
    h = x * (1.0 + scale) + shift
    z = _dot(h.astype(BF16), win_ref[...])

    @pl.when(s == 0)
    def _():
        zext_ref[0:POOL_HALO, :] = jnp.zeros((POOL_HALO, z.shape[1]), F32)

    zext_ref[POOL_HALO:POOL_HALO + ts, :] = z
    pos = s * ts + lax.broadcasted_iota(jnp.int32, (ts, 1), 0)
    gw = z.shape[1] // len(POOL_WINDOWS)
    parts = []
    for gi, win in enumerate(POOL_WINDOWS):
        lo = gi * gw
        zg = z[:, lo:lo + gw]
        acc = zg
        for back in range(1, win):
            acc = acc + zext_ref[POOL_HALO - back:POOL_HALO - back + ts, lo:lo + gw]
        cnt = jnp.minimum(pos + 1, win).astype(F32)
        p = acc / cnt - zg
        parts.append(_dot(p.astype(BF16), wgrp_ref[gi]))
    y = jnp.concatenate(parts, axis=1) * sc_ref[...]
    o = _dot(y.astype(BF16), wout_ref[...])
    zext_ref[0:POOL_HALO, :] = zext_ref[ts:ts + POOL_HALO, :]
    ln = ln_ref[...]
    o_ref[0] = _layer_norm_rows(ALPHA * x + gate * o, ln[0:1], ln[1:2])


def _pool_mix(x, mod_l, w_in, w_grp, scale, w_out, ln_g, ln_b, ts):
    b, s, d = x.shape
    ng, gw, _ = w_grp.shape
    tile = pl.BlockSpec((1, ts, d), lambda i, j: (i, j, 0))
    ln = jnp.zeros((8, d), F32).at[0].set(ln_g).at[1].set(ln_b)
    return pl.pallas_call(
        functools.partial(_pool_kernel, ts=ts),
        name="pool_mix",
        grid=(b, s // ts),
        in_specs=[tile,
                  pl.BlockSpec((1, 6, d), lambda i, j: (i, 0, 0)),
                  pl.BlockSpec((d, d), lambda i, j: (0, 0)),
                  pl.BlockSpec((ng, gw, gw), lambda i, j: (0, 0, 0)),
                  pl.BlockSpec((1, d), lambda i, j: (0, 0)),
                  pl.BlockSpec((d, d), lambda i, j: (0, 0)),
                  pl.BlockSpec((8, d), lambda i, j: (0, 0))],
        out_specs=tile,
        out_shape=jax.ShapeDtypeStruct((b, s, d), F32),
        scratch_shapes=[pltpu.VMEM((ts + POOL_HALO, d), F32)],
        compiler_params=_params(("arbitrary", "arbitrary")),
    )(x, mod_l, w_in.astype(BF16), w_grp.astype(BF16), scale.reshape(1, d), w_out.astype(BF16), ln)


def _fold_kernel(keys_ref, wq_ref, o_ref):
    o_ref[0] = lax.dot_general(keys_ref[0, 0], wq_ref[0], NT_DIMS, precision=HI,
                               preferred_element_type=F32).astype(BF16)


def _peer_fold(keys, w_q):
    depth, nh, two, nk, dk = keys.shape
    d = w_q.shape[1]
    nb = nh * two
    return pl.pallas_call(
        _fold_kernel,
        name="peer_fold",
        grid=(depth, nb),
        in_specs=[pl.BlockSpec((1, 1, nk, dk), lambda l, j: (l, j, 0, 0)),
                  pl.BlockSpec((1, d, dk), lambda l, j: (l, 0, j))],
        out_specs=pl.BlockSpec((1, nk, d), lambda l, j: (l, j, 0)),
        out_shape=jax.ShapeDtypeStruct((depth, nb * nk, d), BF16),
        compiler_params=_params(("arbitrary", "arbitrary")),
    )(keys.reshape(depth, nb, nk, dk), w_q)


N_TOP = PEER_TOPK + 1
CAND = [(a, b) for a in range(N_TOP) for b in range(N_TOP) if (a + 1) * (b + 1) <= N_TOP]
CAND_ROWS = 1 << (len(CAND) - 1).bit_length()


def _sorting_network(n):
    pairs = []

    def merge(lo, cnt, step):
        nxt = step * 2
        if nxt < cnt:
            merge(lo, cnt, nxt)
            merge(lo + step, cnt, nxt)
            for i in range(lo + step, lo + cnt - step, nxt):
                pairs.append((i, i + step))
        else:
            pairs.append((lo, lo + step))

    def sort(lo, cnt):
        if cnt > 1:
            half = cnt // 2
            sort(lo, half)
            sort(lo + half, half)
            merge(lo, cnt, 1)

    sort(0, n)
    return pairs


def _top_sorted(s, n):
    groups = s.shape[0] // SUB
    cols = [s[g * SUB:(g + 1) * SUB, :] for g in range(groups)]
    for i, j in _sorting_network(groups):
        cols[i], cols[j] = jnp.maximum(cols[i], cols[j]), jnp.minimum(cols[i], cols[j])
    sub = lax.broadcasted_iota(jnp.int32, cols[0].shape, 0)
    outs = []
    for it in range(n):
        mx = jnp.max(cols[0], axis=0, keepdims=True)
        outs.append(mx)
        left = n - 1 - it
        if left == 0:
            break
        first = jnp.min(jnp.where(cols[0] == mx, sub, SUB), axis=0, keepdims=True)
        pop = sub == first
        for lvl in range(min(groups, left)):
            below = cols[lvl + 1] if lvl + 1 < groups else NEG_INF
            cols[lvl] = jnp.where(pop, below, cols[lvl])
    return outs


def _route_kernel(x_ref, mod_ref, wf_ref, h2_ref, e2_ref, rk_ref, c1_ref, n1_ref):
    hd = pl.program_id(1)
    m = mod_ref[0]
    shift, scale = m[3:4], m[4:5]

    @pl.when(hd == 0)
    def _():
        h2_ref[...] = (x_ref[...] * (1.0 + scale) + shift).T.astype(BF16)

    h2 = h2_ref[...]
    s1 = _dot(wf_ref[0:N_KEYS, :], h2)
    s2 = _dot(wf_ref[N_KEYS:2 * N_KEYS, :], h2)
    ta = _top_sorted(s1, N_TOP)
    tb = _top_sorted(s2, N_TOP)
    rows = [ta[a] + tb[b] for a, b in CAND]
    rows += [jnp.full_like(rows[0], NEG_INF)] * (CAND_ROWS - len(CAND))
    cand = jnp.concatenate(rows, axis=0)
    tc = _top_sorted(cand, N_TOP)
    tau = 0.5 * (tc[PEER_TOPK - 1] + tc[PEER_TOPK])
    top = ta[0] + tb[0]
    zsum = jnp.sum(jnp.where(cand >= tau, jnp.exp(cand - top), 0.0), axis=0, keepdims=True)
    need = tau - s1
    rank2 = jnp.zeros_like(s2)
    count1 = jnp.zeros_like(s1)
    for b in range(PEER_TOPK):
        rank2 = rank2 + jnp.where(tb[b] > s2, 1.0, 0.0)
        count1 = count1 + jnp.where(tb[b] >= need, 1.0, 0.0)
    e2_ref[0] = jnp.exp(s2 - tb[0]).astype(BF16)
    rk_ref[0] = rank2.astype(BF16)
    c1_ref[0] = jnp.exp(s1 - ta[0]) / zsum
    n1_ref[0] = count1


def _peer_route(x2d, mod_l, wf, tt, tiles_per_batch):
    t, d = x2d.shape
    nh = PEER_HEADS
    per_head = pl.BlockSpec((1, N_KEYS, tt), lambda i, h: (h, 0, i))
    return pl.pallas_call(
        _route_kernel,
        name="peer_route",
        grid=(t // tt, nh),
        in_specs=[pl.BlockSpec((tt, d), lambda i, h: (i, 0)),
                  pl.BlockSpec((1, 6, d), lambda i, h: (i // tiles_per_batch, 0, 0)),
                  pl.BlockSpec((2 * N_KEYS, d), lambda i, h: (h, 0))],
        out_specs=[pl.BlockSpec((d, tt), lambda i, h: (0, i)), per_head, per_head, per_head, per_head],
        out_shape=[jax.ShapeDtypeStruct((d, t), BF16),
                   jax.ShapeDtypeStruct((nh, N_KEYS, t), BF16),
                   jax.ShapeDtypeStruct((nh, N_KEYS, t), BF16),
                   jax.ShapeDtypeStruct((nh, N_KEYS, t), F32),
                   jax.ShapeDtypeStruct((nh, N_KEYS, t), F32)],
        compiler_params=_params(("arbitrary", "arbitrary")),
    )(x2d, mod_l, wf)


ROWS = 16
TOKEN_COLS = 256
MXU_ROWS = 256
MXU_COLS = 256


def _dense_kernel(h2_ref, e2_ref, rk_ref, c1_ref, n1_ref, u_ref, vt_ref, x_ref, mod_ref, ln_ref, o_ref,
                  acc_ref, za_ref, zb_ref, pa_ref, pb_ref, *, eb):
    e = pl.program_id(1)
    last = pl.num_programs(1) - 1
    tt = h2_ref.shape[1]
    slabs = eb // N_KEYS

    @pl.when(e == 0)
    def _():
        acc_ref[...] = jnp.zeros_like(acc_ref)
        zb_ref[...] = jnp.zeros_like(zb_ref)
        pa_ref[...] = jnp.zeros_like(pa_ref)

    def gate_blocks(z_ref, p_ref):
        blocks = []
        for il in range(slabs):
            i1 = jnp.clip((e - 1) * slabs + il, 0, N_KEYS - 1)
            for t0 in range(0, tt, TOKEN_COLS):
                cols = slice(t0, t0 + TOKEN_COLS)
                rows_of = {}

                def token_rows(i1=i1, cols=cols, rows_of=rows_of):
                    if not rows_of:
                        rows_of["cnt"] = [jnp.broadcast_to(n1_ref[hd, pl.ds(i1, 1), cols], (ROWS, TOKEN_COLS)).astype(BF16)
                                          for hd in range(PEER_HEADS)]
                        rows_of["wgt"] = [jnp.broadcast_to(c1_ref[hd, pl.ds(i1, 1), cols], (ROWS, TOKEN_COLS)).astype(BF16)
                                          for hd in range(PEER_HEADS)]
                    return rows_of["cnt"], rows_of["wgt"]

                for jb in range(N_KEYS // ROWS):
                    def block(il=il, jb=jb, cols=cols, token_rows=token_rows):
                        cnt, wgt = token_rows()
                        r0 = il * N_KEYS + jb * ROWS
                        zz = z_ref[r0:r0 + ROWS, cols]
                        act = (0.5 * zz * (1.0 + lax.erf(zz * INV_SQRT2))).astype(BF16)
                        gsum = jnp.zeros((ROWS, TOKEN_COLS), BF16)
                        for hd in range(PEER_HEADS):
                            keep = rk_ref[hd, jb * ROWS:(jb + 1) * ROWS, cols] < cnt[hd]
                            gsum = gsum + jnp.where(keep, e2_ref[hd, jb * ROWS:(jb + 1) * ROWS, cols], 0.0) * wgt[hd]
                        p_ref[r0:r0 + ROWS, cols] = act * gsum
                    blocks.append(block)
        return blocks

    def matmul_units(z_new, p_old):
        units = []
        for n0 in range(0, tt, MXU_COLS):
            for m0 in range(0, acc_ref.shape[0], MXU_ROWS):
                def unit(m0=m0, n0=n0):
                    acc_ref[m0:m0 + MXU_ROWS, n0:n0 + MXU_COLS] += _dot(vt_ref[0, m0:m0 + MXU_ROWS, :], p_old[:, n0:n0 + MXU_COLS])
                units.append((eb, unit))
            for m0 in range(0, eb, MXU_ROWS):
                def unit(m0=m0, n0=n0):
                    z_new[m0:m0 + MXU_ROWS, n0:n0 + MXU_COLS] = _dot(u_ref[m0:m0 + MXU_ROWS, :], h2_ref[:, n0:n0 + MXU_COLS])
                units.append((u_ref.shape[1], unit))
        return units

    def step(z_new, z_old, p_new, p_old):
        blocks = gate_blocks(z_old, p_new)
        units = matmul_units(z_new, p_old)
        total = sum(k for k, _ in units)
        done = 0
        issued = 0
        for k, unit in units:
            unit()
            done += k
            upto = -(-len(blocks) * done // total)
            for block in blocks[issued:upto]:
                block()
            issued = upto

    @pl.when(e % 2 == 0)
    def _():
        step(za_ref, zb_ref, pb_ref, pa_ref)

    @pl.when(e % 2 == 1)
    def _():
        step(zb_ref, za_ref, pa_ref, pb_ref)

    @pl.when(e == last)
    def _():
        m = mod_ref[0]
        gate = m[5:6]
        ln = ln_ref[...]
        y = acc_ref[...].T
        o_ref[...] = _layer_norm_rows(ALPHA * x_ref[...] + gate * y, ln[0:1], ln[1:2])


def _peer_dense(h2, e2, rk2, c1, n1, u_bf, vt_bf, x2d, mod_l, ln_g, ln_b, tt, eb, tiles_per_batch):
    t, d = x2d.shape
    ne = u_bf.shape[0]
    nh = PEER_HEADS
    ln = jnp.zeros((8, d), F32).at[0].set(ln_g).at[1].set(ln_b)
    per_tile = pl.BlockSpec((nh, N_KEYS, tt), lambda i, e: (0, 0, i))
    nslab = ne // eb
    return pl.pallas_call(
        functools.partial(_dense_kernel, eb=eb),
        name="peer_dense",
        grid=(t // tt, nslab + 2),
        in_specs=[pl.BlockSpec((d, tt), lambda i, e: (0, i)),
                  per_tile, per_tile, per_tile, per_tile,
                  pl.BlockSpec((eb, d), lambda i, e: (jnp.minimum(e, nslab - 1), 0)),
                  pl.BlockSpec((1, d, eb), lambda i, e: (jnp.clip(e - 2, 0, nslab - 1), 0, 0)),
                  pl.BlockSpec((tt, d), lambda i, e: (i, 0)),
                  pl.BlockSpec((1, 6, d), lambda i, e: (i // tiles_per_batch, 0, 0)),
                  pl.BlockSpec((8, d), lambda i, e: (0, 0))],
        out_specs=pl.BlockSpec((tt, d), lambda i, e: (i, 0)),
        out_shape=jax.ShapeDtypeStruct((t, d), F32),
        scratch_shapes=[pltpu.VMEM((d, tt), F32),
                        pltpu.VMEM((eb, tt), F32),
                        pltpu.VMEM((eb, tt), F32),
                        pltpu.VMEM((eb, tt), BF16),
                        pltpu.VMEM((eb, tt), BF16)],
        compiler_params=_params(("arbitrary", "arbitrary")),
    )(h2, e2, rk2, c1, n1, u_bf, vt_bf, x2d, mod_l, ln)


def _peer_ffn(x, mod_l, wf, u_tab, v_tab, ln_g, ln_b, tt, eb):
    b, s, d = x.shape
    x2d = x.reshape(b * s, d)
    tpb = s // tt
    h2, e2, rk2, c1, n1 = _peer_route(x2d, mod_l, wf, tt, tpb)
    vt = v_tab.astype(BF16).reshape(v_tab.shape[0] // eb, eb, d).transpose(0, 2, 1)
    out = _peer_dense(h2, e2, rk2, c1, n1, u_tab.astype(BF16), vt, x2d, mod_l, ln_g, ln_b, tt, eb, tpb)
    return out.reshape(b, s, d)


def kernel(x, c, ada_w, ada_b, ln_g, ln_b, rw_mu, rw_w_rkv, rw_w0, rw_w1, rw_w2, rw_a0, rw_a1, rw_a2, rw_g1, rw_g2, rw_k_k, rw_k_a, rw_r_k, rw_lnx_g, rw_lnx_b, rw_w_o, pl_w_in, pl_w_grp, pl_scale, pl_w_out, pe_w_q, pe_keys, pe_u, pe_v):
    b, s, d = x.shape
    depth = ada_w.shape[0]
    ts = min(256, s)
    tc = min(512, s)
    tt = min(512, s)
    eb = 512
    mod = _adaln_mod(c, ada_w, ada_b)
    wf = _peer_fold(pe_keys, pe_w_q)
    for i in range(depth):
        j = i // 2
        if i % 2 == 0:
            r, lw, k, v, kkr, a, g = _rwkv_proj(x, mod[i], rw_mu[j], rw_w_rkv[j], rw_w0[j], rw_w1[j], rw_w2[j],
                                                rw_a0[j], rw_a1[j], rw_a2[j], rw_g1[j], rw_g2[j],
                                                rw_k_k[j], rw_k_a[j], ts)
            y = _wkv_scan(r, lw, k, v, kkr, a, rw_r_k[j], rw_lnx_g[j], rw_lnx_b[j], tc)
            x = _mix_out(y, g, x, mod[i], rw_w_o[j], ln_g[i, 0], ln_b[i, 0], ts)
        else:
            x = _pool_mix(x, mod[i], pl_w_in[j], pl_w_grp[j], pl_scale[j], pl_w_out[j],
                          ln_g[i, 0], ln_b[i, 0], ts)
        x = _peer_ffn(x, mod[i], wf[i], pe_u[i], pe_v[i], ln_g[i, 1], ln_b[i, 1], tt, eb)
    return x
```

```python
import functools
import math

import jax
import jax.numpy as jnp
from jax import lax
from jax.experimental import pallas as pl
from jax.experimental.pallas import tpu as pltpu

F32 = jnp.float32
BF16 = jnp.bfloat16
HI = lax.Precision.HIGHEST

HEAD = 64
PAIR = 2 * HEAD
SUB = 8
CHUNK = 64
GN_EPS = 64e-5
LN_EPS = 1e-5
DEPTH = 2
ALPHA = (2 * DEPTH) ** 0.25
POOL_WINDOWS = (2, 4, 8, 16)
POOL_HALO = 16
N_KEYS = 128
PEER_HEADS = 8
PEER_TOPK = 16
INV_SQRT2 = 0.7071067811865476
NEG_INF = float("-inf")
V7X_VMEM_LIMIT = 56 * 1024 * 1024

NT_DIMS = (((1,), (1,)), ((), ()))
TN_DIMS = (((0,), (0,)), ((), ()))


def _dot(a, b, precision=None):
    return jnp.dot(a, b, precision=precision, preferred_element_type=F32)


def _dot_nt(a, b):
    return lax.dot_general(a, b, NT_DIMS, preferred_element_type=F32)


def _dot_tn(a, b):
    return lax.dot_general(a, b, TN_DIMS, preferred_element_type=F32)


def _split_bf16(x):
    hi = x.astype(BF16)
    return hi, (x - hi.astype(F32)).astype(BF16)


def _dot_sel(x, sel):
    hi, lo = _split_bf16(x)
    sel = sel.astype(BF16)
    return _dot(hi, sel) + _dot(lo, sel)


def _sel_dot(sel, x):
    hi, lo = _split_bf16(x)
    sel = sel.astype(BF16)
    return _dot(sel, hi) + _dot(sel, lo)


def _params(sem, vmem=V7X_VMEM_LIMIT, flags=None):
    return pltpu.CompilerParams(dimension_semantics=sem, vmem_limit_bytes=vmem, flags=flags)


def _layer_norm_rows(z, g, b):
    mu = jnp.mean(z, axis=-1, keepdims=True)
    d = z - mu
    var = jnp.mean(d * d, axis=-1, keepdims=True)
    return d * lax.rsqrt(var + LN_EPS) * g + b


def _mod_kernel(c_ref, w_ref, b_ref, o_ref):
    c = c_ref[...]
    cond = c * jax.nn.sigmoid(c)
    o_ref[0] = _dot(cond, w_ref[0], HI) + b_ref[0]


def _adaln_mod(c, ada_w, ada_b):
    depth, d, nd = ada_w.shape
    b = c.shape[0]
    nmod = nd // d
    out = pl.pallas_call(
        _mod_kernel,
        name="adaln_mod",
        grid=(depth, nmod),
        in_specs=[
            pl.BlockSpec((b, d), lambda l, n: (0, 0)),
            pl.BlockSpec((1, d, d), lambda l, n: (l, 0, n)),
            pl.BlockSpec((1, 1, d), lambda l, n: (l, 0, n)),
        ],
        out_specs=pl.BlockSpec((1, b, d), lambda l, n: (l, 0, n)),
        out_shape=jax.ShapeDtypeStruct((depth, b, nd), F32),
        compiler_params=_params(("arbitrary", "arbitrary")),
    )(c, ada_w, ada_b.reshape(depth, 1, nd))
    return out.reshape(depth, b, nmod, d)


def _softplus(z):
    return jnp.maximum(z, 0.0) + jnp.log1p(jnp.exp(-jnp.abs(z)))


def _rwkv_proj_kernel(x_ref, xp_ref, mod_ref, mu_ref, vec_ref, wrkv_ref, w1_ref, w2_ref, a1_ref, a2_ref,
                      g1_ref, g2_ref, r_ref, lw_ref, k_ref, v_ref, kk_ref, a_ref, g_ref):
    s = pl.program_id(1)
    m = mod_ref[0]
    shift, scale = m[0:1], m[1:2]
    h = x_ref[0] * (1.0 + scale) + shift
    prev = xp_ref[0][7:8] * (1.0 + scale) + shift
    prev = jnp.where(s == 0, 0.0, prev)
    row = lax.broadcasted_iota(jnp.int32, h.shape, 0)
    hprev = jnp.where(row == 0, prev, pltpu.roll(h, 1, 0))
    xx = hprev - h
    mu = mu_ref[...]

    def mix(n):
        return (h + xx * mu[n:n + 1]).astype(BF16)

    vec = vec_ref[...]
    w0, a0, k_k, k_a = vec[0:1], vec[1:2], vec[2:3], vec[3:4]
    r = _dot(mix(0), wrkv_ref[0])
    k = _dot(mix(1), wrkv_ref[1])
    v = _dot(mix(2), wrkv_ref[2])
    wl = w0 + _dot(jnp.tanh(_dot(mix(3), w1_ref[...])).astype(BF16), w2_ref[...])
    w = -_softplus(-wl) - 0.5
    a = jax.nn.sigmoid(a0 + _dot(_dot(mix(4), a1_ref[...]).astype(BF16), a2_ref[...]))
    g = _dot(jax.nn.sigmoid(_dot(mix(5), g1_ref[...])).astype(BF16), g2_ref[...])
    r_ref[0] = r
    lw_ref[0] = -jnp.exp(w)
    k_ref[0] = k * (1.0 + (a - 1.0) * k_a)
    v_ref[0] = v
    kk_ref[0] = k * k_k
    a_ref[0] = a
    g_ref[0] = g.astype(BF16)


def _pad_cols(w, n):
    return jnp.pad(w, ((0, 0), (0, n - w.shape[1])))


def _pad_rows(w, n):
    return jnp.pad(w, ((0, n - w.shape[0]), (0, 0)))


def _rwkv_proj(x, mod_l, mu, w_rkv, w0, w1, w2, a0, a1, a2, g1, g2, k_k, k_a, ts):
    b, s, d = x.shape
    lora = 128
    glora = 256
    vec = jnp.zeros((8, d), F32).at[0].set(w0).at[1].set(a0).at[2].set(k_k).at[3].set(k_a)
    tile = pl.BlockSpec((1, ts, d), lambda i, j: (i, j, 0))
    full2 = lambda shape: pl.BlockSpec(shape, lambda i, j: (0, 0))
    outs = pl.pallas_call(
        _rwkv_proj_kernel,
        name="rwkv_proj",
        grid=(b, s // ts),
        in_specs=[
            tile,
            pl.BlockSpec((1, 8, d), lambda i, j: (i, jnp.maximum(j * (ts // 8) - 1, 0), 0)),
            pl.BlockSpec((1, 6, d), lambda i, j: (i, 0, 0)),
            full2((6, d)),
            full2((8, d)),
            pl.BlockSpec((3, d, d), lambda i, j: (0, 0, 0)),
            full2((d, lora)), full2((lora, d)),
            full2((d, lora)), full2((lora, d)),
            full2((d, glora)), full2((glora, d)),
        ],
        out_specs=[tile] * 7,
        out_shape=[jax.ShapeDtypeStruct((b, s, d), F32)] * 6 + [jax.ShapeDtypeStruct((b, s, d), BF16)],
        compiler_params=_params(("arbitrary", "arbitrary")),
    )(x, x, mod_l, mu, vec, w_rkv.astype(BF16),
      _pad_cols(w1, lora).astype(BF16), _pad_rows(w2, lora).astype(BF16),
      _pad_cols(a1, lora).astype(BF16), _pad_rows(a2, lora).astype(BF16),
      _pad_cols(g1, glora).astype(BF16), _pad_rows(g2, glora).astype(BF16))
    return outs


def _head_masks(shape):
    lane = lax.broadcasted_iota(jnp.int32, shape, len(shape) - 1)
    first = (lane % PAIR) < HEAD
    return first, jnp.logical_not(first)


def _stack_heads(z):
    m0, m1 = _head_masks(z.shape)
    return jnp.concatenate([jnp.where(m0, z, 0.0), jnp.where(m1, z, 0.0)], axis=0)


def _wkv_kernel(r_ref, lw_ref, k_ref, v_ref, kk_ref, a_ref, rk_ref, lng_ref, lnb_ref, y_ref,
                s_ref, yb_ref, *, nchunk):
    L = CHUNK

    @pl.when(pl.program_id(2) == 0)
    def _():
        s_ref[...] = jnp.zeros_like(s_ref)

    r2 = lax.broadcasted_iota(jnp.int32, (PAIR, PAIR), 0)
    c2 = lax.broadcasted_iota(jnp.int32, (PAIR, PAIR), 1)
    same_head = (r2 // HEAD) == (c2 // HEAD)
    eye = r2 == c2
    bd_ones = jnp.where(same_head, 1.0, 0.0).astype(F32)
    tr = lax.broadcasted_iota(jnp.int32, (L, PAIR), 0)
    tc = lax.broadcasted_iota(jnp.int32, (L, PAIR), 1) % HEAD
    strict = tr > tc
    incl = tr >= tc

    def off_diag(size):
        same = (tr // (2 * size)) == (tc // (2 * size))
        return same & ((tr % (2 * size)) >= size) & ((tc % (2 * size)) < size)
    lr = lax.broadcasted_iota(jnp.int32, (L, L), 0)
    lc = lax.broadcasted_iota(jnp.int32, (L, L), 1)
    ltri = jnp.where(lr >= lc, 1.0, 0.0).astype(F32)

    chunks = range(nchunk)
    r_all = r_ref[0]
    lw_all = lw_ref[0]
    k_all = k_ref[0]
    v_all = v_ref[0]
    a_all = a_ref[0]
    kkr = kk_ref[0]
    kk = kkr * lax.rsqrt(jnp.maximum(_dot_sel(kkr * kkr, bd_ones), 1e-24))
    bv_all = kk * a_all

    def rows(x, c):
        return x[c * L:(c + 1) * L]

    cs = [_sel_dot(ltri, rows(lw_all, c)) for c in chunks]
    rt, at, vst, lhs, rhs, bc, kc, p_last = [], [], [], [], [], [], [], []
    for c in chunks:
        cs_last = cs[c][L - 1:L, :]
        pinv = jnp.exp(-cs[c])
        prem = jnp.exp(cs_last - cs[c])
        rt.append(rows(r_all, c) * jnp.exp(cs[c]))
        at.append(-rows(kk, c) * jnp.exp(cs[c] - rows(lw_all, c)))
        bt = rows(bv_all, c) * pinv
        kt = rows(k_all, c) * pinv
        bc.append((rows(bv_all, c) * prem).astype(BF16))
        kc.append((rows(k_all, c) * prem).astype(BF16))
        p_last.append(jnp.exp(cs_last))
        vst.append(_stack_heads(rows(v_all, c)).astype(BF16))
        lhs.append(jnp.concatenate([at[c], rt[c]], axis=0).astype(BF16))
        rhs.append(jnp.concatenate([_stack_heads(bt), _stack_heads(kt)], axis=0).astype(BF16))
    o = [_dot_nt(lhs[c], rhs[c]) for c in chunks]
    nmat = [jnp.where(strict, o[c][:L, :PAIR], 0.0) for c in chunks]
    akv = [_dot(jnp.where(strict, o[c][:L, PAIR:], 0.0).astype(BF16), vst[c]) for c in chunks]
    ident = jnp.where(tr == tc, 1.0, 0.0)
    tinv = [ident + jnp.where(off_diag(1), nmat[c], 0.0) for c in chunks]
    size = 2
    while size < L:
        mask = off_diag(size)
        tn = [_dot(tinv[c].astype(BF16), _stack_heads(jnp.where(mask, nmat[c], 0.0)).astype(BF16)) for c in chunks]
        tinv = [tinv[c] + _dot(tn[c].astype(BF16), _stack_heads(tinv[c]).astype(BF16)) for c in chunks]
        size *= 2
    z = [_dot(tinv[c].astype(BF16), _stack_heads(jnp.concatenate([at[c], akv[c]], axis=1)).astype(BF16))
         for c in chunks]
    tmp = [_dot(jnp.where(incl, o[c][L:, :PAIR], 0.0).astype(BF16), _stack_heads(z[c]).astype(BF16))
           for c in chunks]
    rkv = [_dot(jnp.where(incl, o[c][L:, PAIR:], 0.0).astype(BF16), vst[c]) for c in chunks]
    bz = [_dot_tn(bc[c], z[c].astype(BF16)) for c in chunks]
    kv = [_dot_tn(kc[c], rows(v_all, c).astype(BF16)) for c in chunks]
    state = s_ref[...]
    for c in chunks:
        sb = state.astype(BF16)
        rh = (rt[c] + tmp[c][:, :PAIR]).astype(BF16)
        yb_ref[c * L:(c + 1) * L, :] = _dot(rh, sb) + tmp[c][:, PAIR:] + rkv[c]
        m_c = jnp.where(eye, jnp.broadcast_to(p_last[c], (PAIR, PAIR)), 0.0) + jnp.where(same_head, bz[c][:, :PAIR], 0.0)
        state = _dot(m_c.astype(BF16), sb) + jnp.where(same_head, bz[c][:, PAIR:] + kv[c], 0.0)
    s_ref[...] = state

    y = yb_ref[...]
    bd_avg = bd_ones * (1.0 / HEAD)
    ym = _dot_sel(y, bd_avg)
    d = y - ym
    yv = _dot_sel(d * d, bd_avg)
    yn = d * lax.rsqrt(yv + GN_EPS) * lng_ref[...] + lnb_ref[...]
    bonus = _dot_sel(r_all * k_all * rk_ref[...], bd_ones) * v_all
    y_ref[0] = yn + bonus


def _wkv_scan(r, lw, k, v, kkr, a, r_k, lnx_g, lnx_b, tc):
    b, s, d = r.shape
    nchunk = tc // CHUNK
    tile = pl.BlockSpec((1, tc, PAIR), lambda i, p, j: (i, j, p))
    row = pl.BlockSpec((1, PAIR), lambda i, p, j: (0, p))
    return pl.pallas_call(
        functools.partial(_wkv_kernel, nchunk=nchunk),
        name="wkv_scan",
        grid=(b, d // PAIR, s // tc),
        in_specs=[tile] * 6 + [row] * 3,
        out_specs=tile,
        out_shape=jax.ShapeDtypeStruct((b, s, d), F32),
        scratch_shapes=[
            pltpu.VMEM((PAIR, PAIR), F32),
            pltpu.VMEM((tc, PAIR), F32),
        ],
        compiler_params=_params(("arbitrary", "arbitrary", "arbitrary")),
    )(r, lw, k, v, kkr, a, r_k.reshape(1, d), lnx_g.reshape(1, d), lnx_b.reshape(1, d))


def _mix_out_kernel(y_ref, g_ref, x_ref, mod_ref, wo_ref, ln_ref, o_ref):
    m = mod_ref[0]
    gate = m[2:3]
    yg = (y_ref[0] * g_ref[0].astype(F32)).astype(BF16)
    o = _dot(yg, wo_ref[...])
    z = ALPHA * x_ref[0] + gate * o
    ln = ln_ref[...]
    o_ref[0] = _layer_norm_rows(z, ln[0:1], ln[1:2])


def _mix_out(y, g, x, mod_l, w_o, ln_g, ln_b, ts):
    b, s, d = x.shape
    tile = pl.BlockSpec((1, ts, d), lambda i, j: (i, j, 0))
    ln = jnp.zeros((8, d), F32).at[0].set(ln_g).at[1].set(ln_b)
    return pl.pallas_call(
        _mix_out_kernel,
        name="mix_out",
        grid=(b, s // ts),
        in_specs=[tile, tile, tile,
                  pl.BlockSpec((1, 6, d), lambda i, j: (i, 0, 0)),
                  pl.BlockSpec((d, d), lambda i, j: (0, 0)),
                  pl.BlockSpec((8, d), lambda i, j: (0, 0))],
        out_specs=tile,
        out_shape=jax.ShapeDtypeStruct((b, s, d), F32),
        compiler_params=_params(("arbitrary", "arbitrary")),
    )(y, g, x, mod_l, w_o.astype(BF16), ln)


def _pool_kernel(x_ref, mod_ref, win_ref, wgrp_ref, sc_ref, wout_ref, ln_ref, o_ref, zext_ref, *, ts):
    s = pl.program_id(1)
    m = mod_ref[0]
    shift, scale, gate = m[0:1], m[1:2], m[2:3]
    x = x_ref[0]
    h = x * (1.0 + scale) + shift
    z = _dot(h.astype(BF16), win_ref[...])

    @pl.when(s == 0)
    def _():
        zext_ref[0:POOL_HALO, :] = jnp.zeros((POOL_HALO, z.shape[1]), F32)

    zext_ref[POOL_HALO:POOL_HALO + ts, :] = z
    pos = s * ts + lax.broadcasted_iota(jnp.int32, (ts, 1), 0)
    gw = z.shape[1] // len(POOL_WINDOWS)
    parts = []
    for gi, win in enumerate(POOL_WINDOWS):
        lo = gi * gw
        zg = z[:, lo:lo + gw]
        acc = zg
        for back in range(1, win):
            acc = acc + zext_ref[POOL_HALO - back:POOL_HALO - back + ts, lo:lo + gw]
        cnt = jnp.minimum(pos + 1, win).astype(F32)
        p = acc / cnt - zg
        parts.append(_dot(p.astype(BF16), wgrp_ref[gi]))
    y = jnp.concatenate(parts, axis=1) * sc_ref[...]
    o = _dot(y.astype(BF16), wout_ref[...])
    zext_ref[0:POOL_HALO, :] = zext_ref[ts:ts + POOL_HALO, :]
    ln = ln_ref[...]
    o_ref[0] = _layer_norm_rows(ALPHA * x + gate * o, ln[0:1], ln[1:2])


def _pool_mix(x, mod_l, w_in, w_grp, scale, w_out, ln_g, ln_b, ts):
    b, s, d = x.shape
    ng, gw, _ = w_grp.shape
    tile = pl.BlockSpec((1, ts, d), lambda i, j: (i, j, 0))
    ln = jnp.zeros((8, d), F32).at[0].set(ln_g).at[1].set(ln_b)
    return pl.pallas_call(
        functools.partial(_pool_kernel, ts=ts),
        name="pool_mix",
        grid=(b, s // ts),
        in_specs=[tile,
                  pl.BlockSpec((1, 6, d), lambda i, j: (i, 0, 0)),
                  pl.BlockSpec((d, d), lambda i, j: (0, 0)),
                  pl.BlockSpec((ng, gw, gw), lambda i, j: (0, 0, 0)),
                  pl.BlockSpec((1, d), lambda i, j: (0, 0)),
                  pl.BlockSpec((d, d), lambda i, j: (0, 0)),
                  pl.BlockSpec((8, d), lambda i, j: (0, 0))],
        out_specs=tile,
        out_shape=jax.ShapeDtypeStruct((b, s, d), F32),
        scratch_shapes=[pltpu.VMEM((ts + POOL_HALO, d), F32)],
        compiler_params=_params(("arbitrary", "arbitrary")),
    )(x, mod_l, w_in.astype(BF16), w_grp.astype(BF16), scale.reshape(1, d), w_out.astype(BF16), ln)


def _fold_kernel(keys_ref, wq_ref, o_ref):
    o_ref[0] = lax.dot_general(keys_ref[0, 0], wq_ref[0], NT_DIMS, precision=HI,
                               preferred_element_type=F32).astype(BF16)


def _peer_fold(keys, w_q):
    depth, nh, two, nk, dk = keys.shape
    d = w_q.shape[1]
    nb = nh * two
    return pl.pallas_call(
        _fold_kernel,
        name="peer_fold",
        grid=(depth, nb),
        in_specs=[pl.BlockSpec((1, 1, nk, dk), lambda l, j: (l, j, 0, 0)),
                  pl.BlockSpec((1, d, dk), lambda l, j: (l, 0, j))],
        out_specs=pl.BlockSpec((1, nk, d), lambda l, j: (l, j, 0)),
        out_shape=jax.ShapeDtypeStruct((depth, nb * nk, d), BF16),
        compiler_params=_params(("arbitrary", "arbitrary")),
    )(keys.reshape(depth, nb, nk, dk), w_q)


N_TOP = PEER_TOPK + 1
CAND = [(a, b) for a in range(N_TOP) for b in range(N_TOP) if (a + 1) * (b + 1) <= N_TOP]
CAND_ROWS = 1 << (len(CAND) - 1).bit_length()


def _sorting_network(n):
    pairs = []

    def merge(lo, cnt, step):
        nxt = step * 2
        if nxt < cnt:
            merge(lo, cnt, nxt)
            merge(lo + step, cnt, nxt)
            for i in range(lo + step, lo + cnt - step, nxt):
                pairs.append((i, i + step))
        else:
            pairs.append((lo, lo + step))

    def sort(lo, cnt):
        if cnt > 1:
            half = cnt // 2
            sort(lo, half)
            sort(lo + half, half)
            merge(lo, cnt, 1)

    sort(0, n)
    return pairs


def _top_sorted(s, n):
    groups = s.shape[0] // SUB
    cols = [s[g * SUB:(g + 1) * SUB, :] for g in range(groups)]
    for i, j in _sorting_network(groups):
        cols[i], cols[j] = jnp.maximum(cols[i], cols[j]), jnp.minimum(cols[i], cols[j])
    sub = lax.broadcasted_iota(jnp.int32, cols[0].shape, 0)
    outs = []
    for it in range(n):
        mx = jnp.max(cols[0], axis=0, keepdims=True)
        outs.append(mx)
        left = n - 1 - it
        if left == 0:
            break
        first = jnp.min(jnp.where(cols[0] == mx, sub, SUB), axis=0, keepdims=True)
        pop = sub == first
        for lvl in range(min(groups, left)):
            below = cols[lvl + 1] if lvl + 1 < groups else NEG_INF
            cols[lvl] = jnp.where(pop, below, cols[lvl])
    return outs


def _twin_bf16_words(x):
    bits = pltpu.bitcast(x.astype(BF16).astype(F32), jnp.uint32)
    return bits | (bits >> 16)


def _route_kernel(x_ref, mod_ref, wf_ref, h2_ref, e2_ref, rk_ref, c1_ref, n1_ref):
    hd = pl.program_id(1)
    m = mod_ref[0]
    shift, scale = m[3:4], m[4:5]

    @pl.when(hd == 0)
    def _():
        h2_ref[...] = (x_ref[...] * (1.0 + scale) + shift).T.astype(BF16)

    h2 = h2_ref[...]
    s1 = _dot(wf_ref[0:N_KEYS, :], h2)
    s2 = _dot(wf_ref[N_KEYS:2 * N_KEYS, :], h2)
    ta = _top_sorted(s1, N_TOP)
    tb = _top_sorted(s2, N_TOP)
    rows = [ta[a] + tb[b] for a, b in CAND]
    rows += [jnp.full_like(rows[0], NEG_INF)] * (CAND_ROWS - len(CAND))
    cand = jnp.concatenate(rows, axis=0)
    tc = _top_sorted(cand, N_TOP)
    tau = 0.5 * (tc[PEER_TOPK - 1] + tc[PEER_TOPK])
    top = ta[0] + tb[0]
    zsum = jnp.sum(jnp.where(cand >= tau, jnp.exp(cand - top), 0.0), axis=0, keepdims=True)
    need = tau - s1
    rank2 = jnp.zeros_like(s2)
    count1 = jnp.zeros_like(s1)
    for b in range(PEER_TOPK):
        rank2 = rank2 + jnp.where(tb[b] > s2, 1.0, 0.0)
        count1 = count1 + jnp.where(tb[b] >= need, 1.0, 0.0)
    e2_ref[0] = jnp.exp(s2 - tb[0]).astype(BF16)
    rk_ref[0] = rank2.astype(BF16)
    c1_ref[0] = _twin_bf16_words(jnp.exp(s1 - ta[0]) / zsum)
    n1_ref[0] = _twin_bf16_words(count1)


def _peer_route(x2d, mod_l, wf, tt, tiles_per_batch):
    t, d = x2d.shape
    nh = PEER_HEADS
    per_head = pl.BlockSpec((1, N_KEYS, tt), lambda i, h: (h, 0, i))
    return pl.pallas_call(
        _route_kernel,
        name="peer_route",
        grid=(t // tt, nh),
        in_specs=[pl.BlockSpec((tt, d), lambda i, h: (i, 0)),
                  pl.BlockSpec((1, 6, d), lambda i, h: (i // tiles_per_batch, 0, 0)),
                  pl.BlockSpec((2 * N_KEYS, d), lambda i, h: (h, 0))],
        out_specs=[pl.BlockSpec((d, tt), lambda i, h: (0, i)), per_head, per_head, per_head, per_head],
        out_shape=[jax.ShapeDtypeStruct((d, t), BF16),
                   jax.ShapeDtypeStruct((nh, N_KEYS, t), BF16),
                   jax.ShapeDtypeStruct((nh, N_KEYS, t), BF16),
                   jax.ShapeDtypeStruct((nh, N_KEYS, t), jnp.uint32),
                   jax.ShapeDtypeStruct((nh, N_KEYS, t), jnp.uint32)],
        compiler_params=_params(("arbitrary", "arbitrary")),
    )(x2d, mod_l, wf)


ROWS = 16
TOKEN_COLS = 256
MXU_ROWS = 256
MXU_COLS = 256


def _dense_kernel(h2_ref, e2_ref, rk_ref, c1_ref, n1_ref, u_ref, vt_ref, x_ref, mod_ref, ln_ref, o_ref,
                  acc_ref, za_ref, zb_ref, pa_ref, pb_ref, *, eb):
    e = pl.program_id(1)
    last = pl.num_programs(1) - 1
    tt = h2_ref.shape[1]
    slabs = eb // N_KEYS

    @pl.when(e == 0)
    def _():
        acc_ref[...] = jnp.zeros_like(acc_ref)
        zb_ref[...] = jnp.zeros_like(zb_ref)
        pa_ref[...] = jnp.zeros_like(pa_ref)

    def gate_blocks(z_ref, p_ref):
        blocks = []
        for il in range(slabs):
            i1 = jnp.clip((e - 1) * slabs + il, 0, N_KEYS - 1)
            for t0 in range(0, tt, TOKEN_COLS):
                cols = slice(t0, t0 + TOKEN_COLS)
                rows_of = {}

                def token_rows(i1=i1, cols=cols, rows_of=rows_of):
                    if not rows_of:
                        def tile_of(ref, hd):
                            words = jnp.broadcast_to(ref[hd, pl.ds(i1, 1), cols], (ROWS // 2, TOKEN_COLS))
                            return pltpu.bitcast(words, BF16)
                        rows_of["cnt"] = [tile_of(n1_ref, hd) for hd in range(PEER_HEADS)]
                        rows_of["wgt"] = [tile_of(c1_ref, hd) for hd in range(PEER_HEADS)]
                    return rows_of["cnt"], rows_of["wgt"]

                for jb in range(N_KEYS // ROWS):
                    def block(il=il, jb=jb, cols=cols, token_rows=token_rows):
                        cnt, wgt = token_rows()
                        r0 = il * N_KEYS + jb * ROWS
                        zz = z_ref[r0:r0 + ROWS, cols].astype(BF16)
                        act = 0.5 * zz * (1.0 + lax.erf(zz * INV_SQRT2))
                        gsum = jnp.zeros((ROWS, TOKEN_COLS), BF16)
                        for hd in range(PEER_HEADS):
                            keep = rk_ref[hd, jb * ROWS:(jb + 1) * ROWS, cols] < cnt[hd]
                            gsum = gsum + jnp.where(keep, e2_ref[hd, jb * ROWS:(jb + 1) * ROWS, cols], 0.0) * wgt[hd]
                        p_ref[r0:r0 + ROWS, cols] = act * gsum
                    blocks.append(block)
        return blocks

    def matmul_units(z_new, p_old):
        units = []
        for n0 in range(0, tt, MXU_COLS):
            for m0 in range(0, acc_ref.shape[0], MXU_ROWS):
                def unit(m0=m0, n0=n0):
                    acc_ref[m0:m0 + MXU_ROWS, n0:n0 + MXU_COLS] += _dot(vt_ref[0, m0:m0 + MXU_ROWS, :], p_old[:, n0:n0 + MXU_COLS])
                units.append((eb, unit))
            for m0 in range(0, eb, MXU_ROWS):
                def unit(m0=m0, n0=n0):
                    z_new[m0:m0 + MXU_ROWS, n0:n0 + MXU_COLS] = _dot(u_ref[m0:m0 + MXU_ROWS, :], h2_ref[:, n0:n0 + MXU_COLS])
                units.append((u_ref.shape[1], unit))
        return units

    def step(z_new, z_old, p_new, p_old):
        blocks = gate_blocks(z_old, p_new)
        units = matmul_units(z_new, p_old)
        total = sum(k for k, _ in units)
        done = 0
        issued = 0
        for k, unit in units:
            unit()
            done += k
            upto = -(-len(blocks) * done // total)
            for block in blocks[issued:upto]:
                block()
            issued = upto

    @pl.when(e % 2 == 0)
    def _():
        step(za_ref, zb_ref, pb_ref, pa_ref)

    @pl.when(e % 2 == 1)
    def _():
        step(zb_ref, za_ref, pa_ref, pb_ref)

    @pl.when(e == last)
    def _():
        m = mod_ref[0]
        gate = m[5:6]
        ln = ln_ref[...]
        y = acc_ref[...].T
        o_ref[...] = _layer_norm_rows(ALPHA * x_ref[...] + gate * y, ln[0:1], ln[1:2])


def _peer_dense(h2, e2, rk2, c1, n1, u_bf, vt_bf, x2d, mod_l, ln_g, ln_b, tt, eb, tiles_per_batch):
    t, d = x2d.shape
    nh = PEER_HEADS
    ln = jnp.zeros((8, d), F32).at[0].set(ln_g).at[1].set(ln_b)
    nslab = u_bf.shape[0] // eb
    per_tile = pl.BlockSpec((nh, N_KEYS, tt), lambda i, e: (0, 0, i))
    return pl.pallas_call(
        functools.partial(_dense_kernel, eb=eb),
        name="peer_dense",
        grid=(t // tt, nslab + 2),
        in_specs=[pl.BlockSpec((d, tt), lambda i, e: (0, i)),
                  per_tile, per_tile, per_tile, per_tile,
                  pl.BlockSpec((eb, d), lambda i, e: (jnp.minimum(e, nslab - 1), 0)),
                  pl.BlockSpec((1, d, eb), lambda i, e: (jnp.clip(e - 2, 0, nslab - 1), 0, 0)),
                  pl.BlockSpec((tt, d), lambda i, e: (i, 0)),
                  pl.BlockSpec((1, 6, d), lambda i, e: (i // tiles_per_batch, 0, 0)),
                  pl.BlockSpec((8, d), lambda i, e: (0, 0))],
        out_specs=pl.BlockSpec((tt, d), lambda i, e: (i, 0)),
        out_shape=jax.ShapeDtypeStruct((t, d), F32),
        scratch_shapes=[pltpu.VMEM((d, tt), F32),
                        pltpu.VMEM((eb, tt), F32),
                        pltpu.VMEM((eb, tt), F32),
                        pltpu.VMEM((eb, tt), BF16),
                        pltpu.VMEM((eb, tt), BF16)],
        compiler_params=_params(("arbitrary", "arbitrary")),
    )(h2, e2, rk2, c1, n1, u_bf, vt_bf, x2d, mod_l, ln)


def _peer_ffn(x, mod_l, wf, u_tab, v_tab, ln_g, ln_b, tt_route, tt_dense, eb):
    b, s, d = x.shape
    x2d = x.reshape(b * s, d)
    h2, e2, rk2, c1, n1 = _peer_route(x2d, mod_l, wf, tt_route, s // tt_route)
    vt = v_tab.astype(BF16).reshape(v_tab.shape[0] // eb, eb, d).transpose(0, 2, 1)
    out = _peer_dense(h2, e2, rk2, c1, n1, u_tab.astype(BF16), vt, x2d, mod_l, ln_g, ln_b,
                      tt_dense, eb, s // tt_dense)
    return out.reshape(b, s, d)


def kernel(x, c, ada_w, ada_b, ln_g, ln_b, rw_mu, rw_w_rkv, rw_w0, rw_w1, rw_w2, rw_a0, rw_a1, rw_a2, rw_g1, rw_g2, rw_k_k, rw_k_a, rw_r_k, rw_lnx_g, rw_lnx_b, rw_w_o, pl_w_in, pl_w_grp, pl_scale, pl_w_out, pe_w_q, pe_keys, pe_u, pe_v):
    b, s, d = x.shape
    depth = ada_w.shape[0]
    ts = min(256, s)
    tc = min(1024, s)
    tt = min(512, s)
    tt_dense = min(512, s)
    eb = 512
    mod = _adaln_mod(c, ada_w, ada_b)
    wf = _peer_fold(pe_keys, pe_w_q)
    for i in range(depth):
        j = i // 2
        if i % 2 == 0:
            r, lw, k, v, kkr, a, g = _rwkv_proj(x, mod[i], rw_mu[j], rw_w_rkv[j], rw_w0[j], rw_w1[j], rw_w2[j],
                                                rw_a0[j], rw_a1[j], rw_a2[j], rw_g1[j], rw_g2[j],
                                                rw_k_k[j], rw_k_a[j], ts)
            y = _wkv_scan(r, lw, k, v, kkr, a, rw_r_k[j], rw_lnx_g[j], rw_lnx_b[j], tc)
            x = _mix_out(y, g, x, mod[i], rw_w_o[j], ln_g[i, 0], ln_b[i, 0], ts)
        else:
            x = _pool_mix(x, mod[i], pl_w_in[j], pl_w_grp[j], pl_scale[j], pl_w_out[j],
                          ln_g[i, 0], ln_b[i, 0], ts)
        x = _peer_ffn(x, mod[i], wf[i], pe_u[i], pe_v[i], ln_g[i, 1], ln_b[i, 1], tt, tt_dense, eb)
    return x
```

```python
import functools
import math

import jax
import jax.numpy as jnp
from jax import lax
from jax.experimental import pallas as pl
from jax.experimental.pallas import tpu as pltpu

F32 = jnp.float32
BF16 = jnp.bfloat16
HI = lax.Precision.HIGHEST

HEAD = 64
PAIR = 2 * HEAD
SUB = 8
LANES = 128
CHUNK = 64
GN_EPS = 64e-5
LN_EPS = 1e-5
DEPTH = 2
ALPHA = (2 * DEPTH) ** 0.25
POOL_WINDOWS = (2, 4, 8, 16)
POOL_HALO = 16
N_KEYS = 128
PEER_HEADS = 8
PEER_TOPK = 16
INV_SQRT2 = 0.7071067811865476
NEG_INF = float("-inf")
V7X_VMEM_LIMIT = 56 * 1024 * 1024

NT_DIMS = (((1,), (1,)), ((), ()))
TN_DIMS = (((0,), (0,)), ((), ()))


def _dot(a, b, precision=None):
    return jnp.dot(a, b, precision=precision, preferred_element_type=F32)


def _dot_nt(a, b):
    return lax.dot_general(a, b, NT_DIMS, preferred_element_type=F32)


def _dot_tn(a, b):
    return lax.dot_general(a, b, TN_DIMS, preferred_element_type=F32)


def _split_bf16(x):
    hi = x.astype(BF16)
    return hi, (x - hi.astype(F32)).astype(BF16)


def _dot_sel(x, sel):
    hi, lo = _split_bf16(x)
    sel = sel.astype(BF16)
    return _dot(hi, sel) + _dot(lo, sel)


def _sel_dot(sel, x):
    hi, lo = _split_bf16(x)
    sel = sel.astype(BF16)
    return _dot(sel, hi) + _dot(sel, lo)


def _params(sem, vmem=V7X_VMEM_LIMIT, flags=None):
    return pltpu.CompilerParams(dimension_semantics=sem, vmem_limit_bytes=vmem, flags=flags)


def _layer_norm_rows(z, g, b):
    mu = jnp.mean(z, axis=-1, keepdims=True)
    d = z - mu
    var = jnp.mean(d * d, axis=-1, keepdims=True)
    return d * lax.rsqrt(var + LN_EPS) * g + b


def _mod_kernel(c_ref, w_ref, b_ref, o_ref):
    c = c_ref[...]
    cond = c * jax.nn.sigmoid(c)
    o_ref[0] = _dot(cond, w_ref[0], HI) + b_ref[0]


def _adaln_mod(c, ada_w, ada_b):
    depth, d, nd = ada_w.shape
    b = c.shape[0]
    nmod = nd // d
    out = pl.pallas_call(
        _mod_kernel,
        name="adaln_mod",
        grid=(depth, nmod),
        in_specs=[
            pl.BlockSpec((b, d), lambda l, n: (0, 0)),
            pl.BlockSpec((1, d, d), lambda l, n: (l, 0, n)),
            pl.BlockSpec((1, 1, d), lambda l, n: (l, 0, n)),
        ],
        out_specs=pl.BlockSpec((1, b, d), lambda l, n: (l, 0, n)),
        out_shape=jax.ShapeDtypeStruct((depth, b, nd), F32),
        compiler_params=_params(("arbitrary", "arbitrary")),
    )(c, ada_w, ada_b.reshape(depth, 1, nd))
    return out.reshape(depth, b, nmod, d)


def _softplus(z):
    return jnp.maximum(z, 0.0) + jnp.log1p(jnp.exp(-jnp.abs(z)))


def _rwkv_proj_kernel(x_ref, xp_ref, mod_ref, mu_ref, vec_ref, wrkv_ref, w1_ref, w2_ref, a1_ref, a2_ref,
                      g1_ref, g2_ref, r_ref, lw_ref, k_ref, v_ref, kk_ref, a_ref, g_ref):
    s = pl.program_id(1)
    m = mod_ref[0]
    shift, scale = m[0:1], m[1:2]
    h = x_ref[0] * (1.0 + scale) + shift
    prev = xp_ref[0][7:8] * (1.0 + scale) + shift
    prev = jnp.where(s == 0, 0.0, prev)
    row = lax.broadcasted_iota(jnp.int32, h.shape, 0)
    hprev = jnp.where(row == 0, prev, pltpu.roll(h, 1, 0))
    xx = hprev - h
    mu = mu_ref[...]

    def mix(n):
        return (h + xx * mu[n:n + 1]).astype(BF16)

    vec = vec_ref[...]
    w0, a0, k_k, k_a = vec[0:1], vec[1:2], vec[2:3], vec[3:4]
    r = _dot(mix(0), wrkv_ref[0])
    k = _dot(mix(1), wrkv_ref[1])
    v = _dot(mix(2), wrkv_ref[2])
    wl = w0 + _dot(jnp.tanh(_dot(mix(3), w1_ref[...])).astype(BF16), w2_ref[...])
    w = -_softplus(-wl) - 0.5
    a = jax.nn.sigmoid(a0 + _dot(_dot(mix(4), a1_ref[...]).astype(BF16), a2_ref[...]))
    g = _dot(jax.nn.sigmoid(_dot(mix(5), g1_ref[...])).astype(BF16), g2_ref[...])
    r_ref[0] = r
    lw_ref[0] = -jnp.exp(w)
    k_ref[0] = k * (1.0 + (a - 1.0) * k_a)
    v_ref[0] = v
    kk_ref[0] = k * k_k
    a_ref[0] = a
    g_ref[0] = g.astype(BF16)


def _pad_cols(w, n):
    return jnp.pad(w, ((0, 0), (0, n - w.shape[1])))


def _pad_rows(w, n):
    return jnp.pad(w, ((0, n - w.shape[0]), (0, 0)))


def _rwkv_proj(x, mod_l, mu, w_rkv, w0, w1, w2, a0, a1, a2, g1, g2, k_k, k_a, ts):
    b, s, d = x.shape
    lora = 128
    glora = 256
    vec = jnp.zeros((8, d), F32).at[0].set(w0).at[1].set(a0).at[2].set(k_k).at[3].set(k_a)
    tile = pl.BlockSpec((1, ts, d), lambda i, j: (i, j, 0))
    full2 = lambda shape: pl.BlockSpec(shape, lambda i, j: (0, 0))
    outs = pl.pallas_call(
        _rwkv_proj_kernel,
        name="rwkv_proj",
        grid=(b, s // ts),
        in_specs=[
            tile,
            pl.BlockSpec((1, 8, d), lambda i, j: (i, jnp.maximum(j * (ts // 8) - 1, 0), 0)),
            pl.BlockSpec((1, 6, d), lambda i, j: (i, 0, 0)),
            full2((6, d)),
            full2((8, d)),
            pl.BlockSpec((3, d, d), lambda i, j: (0, 0, 0)),
            full2((d, lora)), full2((lora, d)),
            full2((d, lora)), full2((lora, d)),
            full2((d, glora)), full2((glora, d)),
        ],
        out_specs=[tile] * 7,
        out_shape=[jax.ShapeDtypeStruct((b, s, d), F32)] * 6 + [jax.ShapeDtypeStruct((b, s, d), BF16)],
        compiler_params=_params(("arbitrary", "arbitrary")),
    )(x, x, mod_l, mu, vec, w_rkv.astype(BF16),
      _pad_cols(w1, lora).astype(BF16), _pad_rows(w2, lora).astype(BF16),
      _pad_cols(a1, lora).astype(BF16), _pad_rows(a2, lora).astype(BF16),
      _pad_cols(g1, glora).astype(BF16), _pad_rows(g2, glora).astype(BF16))
    return outs


def _head_masks(shape):
    lane = lax.broadcasted_iota(jnp.int32, shape, len(shape) - 1)
    first = (lane % PAIR) < HEAD
    return first, jnp.logical_not(first)


def _stack_heads(z):
    m0, m1 = _head_masks(z.shape)
    return jnp.concatenate([jnp.where(m0, z, 0.0), jnp.where(m1, z, 0.0)], axis=0)


def _wkv_kernel(r_ref, lw_ref, k_ref, v_ref, kk_ref, a_ref, rk_ref, lng_ref, lnb_ref, y_ref,
                s_ref, yb_ref, *, nchunk):
    L = CHUNK

    @pl.when(pl.program_id(2) == 0)
    def _():
        s_ref[...] = jnp.zeros_like(s_ref)

    r2 = lax.broadcasted_iota(jnp.int32, (PAIR, PAIR), 0)
    c2 = lax.broadcasted_iota(jnp.int32, (PAIR, PAIR), 1)
    same_head = (r2 // HEAD) == (c2 // HEAD)
    eye = r2 == c2
    bd_ones = jnp.where(same_head, 1.0, 0.0).astype(F32)
    tr = lax.broadcasted_iota(jnp.int32, (L, PAIR), 0)
    tc = lax.broadcasted_iota(jnp.int32, (L, PAIR), 1) % HEAD
    strict = tr > tc
    incl = tr >= tc

    def off_diag(size):
        same = (tr // (2 * size)) == (tc // (2 * size))
        return same & ((tr % (2 * size)) >= size) & ((tc % (2 * size)) < size)
    lr = lax.broadcasted_iota(jnp.int32, (L, L), 0)
    lc = lax.broadcasted_iota(jnp.int32, (L, L), 1)
    ltri = jnp.where(lr >= lc, 1.0, 0.0).astype(F32)

    chunks = range(nchunk)
    r_all = r_ref[0]
    lw_all = lw_ref[0]
    k_all = k_ref[0]
    v_all = v_ref[0]
    a_all = a_ref[0]
    kkr = kk_ref[0]
    kk = kkr * lax.rsqrt(jnp.maximum(_dot_sel(kkr * kkr, bd_ones), 1e-24))
    bv_all = kk * a_all

    def rows(x, c):
        return x[c * L:(c + 1) * L]

    cs = [_sel_dot(ltri, rows(lw_all, c)) for c in chunks]
    rt, at, vst, lhs, rhs, bc, kc, p_last = [], [], [], [], [], [], [], []
    for c in chunks:
        cs_last = cs[c][L - 1:L, :]
        pinv = jnp.exp(-cs[c])
        prem = jnp.exp(cs_last - cs[c])
        rt.append(rows(r_all, c) * jnp.exp(cs[c]))
        at.append(-rows(kk, c) * jnp.exp(cs[c] - rows(lw_all, c)))
        bt = rows(bv_all, c) * pinv
        kt = rows(k_all, c) * pinv
        bc.append((rows(bv_all, c) * prem).astype(BF16))
        kc.append((rows(k_all, c) * prem).astype(BF16))
        p_last.append(jnp.exp(cs_last))
        vst.append(_stack_heads(rows(v_all, c)).astype(BF16))
        lhs.append(jnp.concatenate([at[c], rt[c]], axis=0).astype(BF16))
        rhs.append(jnp.concatenate([_stack_heads(bt), _stack_heads(kt)], axis=0).astype(BF16))
    o = [_dot_nt(lhs[c], rhs[c]) for c in chunks]
    nmat = [jnp.where(strict, o[c][:L, :PAIR], 0.0) for c in chunks]
    akv = [_dot(jnp.where(strict, o[c][:L, PAIR:], 0.0).astype(BF16), vst[c]) for c in chunks]
    ident = jnp.where(tr == tc, 1.0, 0.0)
    tinv = [ident + jnp.where(off_diag(1), nmat[c], 0.0) for c in chunks]
    size = 2
    while size < L:
        mask = off_diag(size)
        tn = [_dot(tinv[c].astype(BF16), _stack_heads(jnp.where(mask, nmat[c], 0.0)).astype(BF16)) for c in chunks]
        tinv = [tinv[c] + _dot(tn[c].astype(BF16), _stack_heads(tinv[c]).astype(BF16)) for c in chunks]
        size *= 2
    z = [_dot(tinv[c].astype(BF16), _stack_heads(jnp.concatenate([at[c], akv[c]], axis=1)).astype(BF16))
         for c in chunks]
    tmp = [_dot(jnp.where(incl, o[c][L:, :PAIR], 0.0).astype(BF16), _stack_heads(z[c]).astype(BF16))
           for c in chunks]
    rkv = [_dot(jnp.where(incl, o[c][L:, PAIR:], 0.0).astype(BF16), vst[c]) for c in chunks]
    bz = [_dot_tn(bc[c], z[c].astype(BF16)) for c in chunks]
    kv = [_dot_tn(kc[c], rows(v_all, c).astype(BF16)) for c in chunks]
    state = s_ref[...]
    for c in chunks:
        sb = state.astype(BF16)
        rh = (rt[c] + tmp[c][:, :PAIR]).astype(BF16)
        yb_ref[c * L:(c + 1) * L, :] = _dot(rh, sb) + tmp[c][:, PAIR:] + rkv[c]
        m_c = jnp.where(eye, jnp.broadcast_to(p_last[c], (PAIR, PAIR)), 0.0) + jnp.where(same_head, bz[c][:, :PAIR], 0.0)
        state = _dot(m_c.astype(BF16), sb) + jnp.where(same_head, bz[c][:, PAIR:] + kv[c], 0.0)
    s_ref[...] = state

    y = yb_ref[...]
    bd_avg = bd_ones * (1.0 / HEAD)
    ym = _dot_sel(y, bd_avg)
    d = y - ym
    yv = _dot_sel(d * d, bd_avg)
    yn = d * lax.rsqrt(yv + GN_EPS) * lng_ref[...] + lnb_ref[...]
    bonus = _dot_sel(r_all * k_all * rk_ref[...], bd_ones) * v_all
    y_ref[0] = yn + bonus


def _wkv_scan(r, lw, k, v, kkr, a, r_k, lnx_g, lnx_b, tc):
    b, s, d = r.shape
    nchunk = tc // CHUNK
    tile = pl.BlockSpec((1, tc, PAIR), lambda i, p, j: (i, j, p))
    row = pl.BlockSpec((1, PAIR), lambda i, p, j: (0, p))
    return pl.pallas_call(
        functools.partial(_wkv_kernel, nchunk=nchunk),
        name="wkv_scan",
        grid=(b, d // PAIR, s // tc),
        in_specs=[tile] * 6 + [row] * 3,
        out_specs=tile,
        out_shape=jax.ShapeDtypeStruct((b, s, d), F32),
        scratch_shapes=[
            pltpu.VMEM((PAIR, PAIR), F32),
            pltpu.VMEM((tc, PAIR), F32),
        ],
        compiler_params=_params(("arbitrary", "arbitrary", "arbitrary")),
    )(r, lw, k, v, kkr, a, r_k.reshape(1, d), lnx_g.reshape(1, d), lnx_b.reshape(1, d))


def _mix_out_kernel(y_ref, g_ref, x_ref, mod_ref, wo_ref, ln_ref, o_ref):
    m = mod_ref[0]
    gate = m[2:3]
    yg = (y_ref[0] * g_ref[0].astype(F32)).astype(BF16)
    o = _dot(yg, wo_ref[...])
    z = ALPHA * x_ref[0] + gate * o
    ln = ln_ref[...]
    o_ref[0] = _layer_norm_rows(z, ln[0:1], ln[1:2])


def _mix_out(y, g, x, mod_l, w_o, ln_g, ln_b, ts):
    b, s, d = x.shape
    tile = pl.BlockSpec((1, ts, d), lambda i, j: (i, j, 0))
    ln = jnp.zeros((8, d), F32).at[0].set(ln_g).at[1].set(ln_b)
    return pl.pallas_call(
        _mix_out_kernel,
        name="mix_out",
        grid=(b, s // ts),
        in_specs=[tile, tile, tile,
                  pl.BlockSpec((1, 6, d), lambda i, j: (i, 0, 0)),
                  pl.BlockSpec((d, d), lambda i, j: (0, 0)),
                  pl.BlockSpec((8, d), lambda i, j: (0, 0))],
        out_specs=tile,
        out_shape=jax.ShapeDtypeStruct((b, s, d), F32),
        compiler_params=_params(("arbitrary", "arbitrary")),
    )(y, g, x, mod_l, w_o.astype(BF16), ln)


def _pool_kernel(x_ref, mod_ref, win_ref, wgrp_ref, sc_ref, wout_ref, ln_ref, o_ref, zext_ref, *, ts):
    s = pl.program_id(1)
    m = mod_ref[0]
    shift, scale, gate = m[0:1], m[1:2], m[2:3]
    x = x_ref[0]
    h = x * (1.0 + scale) + shift
    z = _dot(h.astype(BF16), win_ref[...])

    @pl.when(s == 0)
    def _():
        zext_ref[0:POOL_HALO, :] = jnp.zeros((POOL_HALO, z.shape[1]), F32)

    zext_ref[POOL_HALO:POOL_HALO + ts, :] = z
    pos = s * ts + lax.broadcasted_iota(jnp.int32, (ts, 1), 0)
    gw = z.shape[1] // len(POOL_WINDOWS)
    parts = []
    for gi, win in enumerate(POOL_WINDOWS):
        lo = gi * gw
        zg = z[:, lo:lo + gw]
        acc = zg
        for back in range(1, win):
            acc = acc + zext_ref[POOL_HALO - back:POOL_HALO - back + ts, lo:lo + gw]
        cnt = jnp.minimum(pos + 1, win).astype(F32)
        p = acc / cnt - zg
        parts.append(_dot(p.astype(BF16), wgrp_ref[gi]))
    y = jnp.concatenate(parts, axis=1) * sc_ref[...]
    o = _dot(y.astype(BF16), wout_ref[...])
    zext_ref[0:POOL_HALO, :] = zext_ref[ts:ts + POOL_HALO, :]
    ln = ln_ref[...]
    o_ref[0] = _layer_norm_rows(ALPHA * x + gate * o, ln[0:1], ln[1:2])


def _pool_mix(x, mod_l, w_in, w_grp, scale, w_out, ln_g, ln_b, ts):
    b, s, d = x.shape
    ng, gw, _ = w_grp.shape
    tile = pl.BlockSpec((1, ts, d), lambda i, j: (i, j, 0))
    ln = jnp.zeros((8, d), F32).at[0].set(ln_g).at[1].set(ln_b)
    return pl.pallas_call(
        functools.partial(_pool_kernel, ts=ts),
        name="pool_mix",
        grid=(b, s // ts),
        in_specs=[tile,
                  pl.BlockSpec((1, 6, d), lambda i, j: (i, 0, 0)),
                  pl.BlockSpec((d, d), lambda i, j: (0, 0)),
                  pl.BlockSpec((ng, gw, gw), lambda i, j: (0, 0, 0)),
                  pl.BlockSpec((1, d), lambda i, j: (0, 0)),
                  pl.BlockSpec((d, d), lambda i, j: (0, 0)),
                  pl.BlockSpec((8, d), lambda i, j: (0, 0))],
        out_specs=tile,
        out_shape=jax.ShapeDtypeStruct((b, s, d), F32),
        scratch_shapes=[pltpu.VMEM((ts + POOL_HALO, d), F32)],
        compiler_params=_params(("arbitrary", "arbitrary")),
    )(x, mod_l, w_in.astype(BF16), w_grp.astype(BF16), scale.reshape(1, d), w_out.astype(BF16), ln)


def _fold_kernel(keys_ref, wq_ref, o_ref):
    o_ref[0] = lax.dot_general(keys_ref[0, 0], wq_ref[0], NT_DIMS, precision=HI,
                               preferred_element_type=F32).astype(BF16)


def _peer_fold(keys, w_q):
    depth, nh, two, nk, dk = keys.shape
    d = w_q.shape[1]
    nb = nh * two
    return pl.pallas_call(
        _fold_kernel,
        name="peer_fold",
        grid=(depth, nb),
        in_specs=[pl.BlockSpec((1, 1, nk, dk), lambda l, j: (l, j, 0, 0)),
                  pl.BlockSpec((1, d, dk), lambda l, j: (l, 0, j))],
        out_specs=pl.BlockSpec((1, nk, d), lambda l, j: (l, j, 0)),
        out_shape=jax.ShapeDtypeStruct((depth, nb * nk, d), BF16),
        compiler_params=_params(("arbitrary", "arbitrary")),
    )(keys.reshape(depth, nb, nk, dk), w_q)


N_TOP = PEER_TOPK + 1
CAND = [(a, b) for a in range(N_TOP) for b in range(N_TOP) if (a + 1) * (b + 1) <= N_TOP]
CAND_ROWS = 1 << (len(CAND) - 1).bit_length()


def _sorting_network(n):
    pairs = []

    def merge(lo, cnt, step):
        nxt = step * 2
        if nxt < cnt:
            merge(lo, cnt, nxt)
            merge(lo + step, cnt, nxt)
            for i in range(lo + step, lo + cnt - step, nxt):
                pairs.append((i, i + step))
        else:
            pairs.append((lo, lo + step))

    def sort(lo, cnt):
        if cnt > 1:
            half = cnt // 2
            sort(lo, half)
            sort(lo + half, half)
            merge(lo, cnt, 1)

    sort(0, n)
    return pairs


def _top_sorted(s, n):
    groups = s.shape[0] // SUB
    cols = [s[g * SUB:(g + 1) * SUB, :] for g in range(groups)]
    for i, j in _sorting_network(groups):
        cols[i], cols[j] = jnp.maximum(cols[i], cols[j]), jnp.minimum(cols[i], cols[j])
    sub = lax.broadcasted_iota(jnp.int32, cols[0].shape, 0)
    outs = []
    for it in range(n):
        mx = jnp.max(cols[0], axis=0, keepdims=True)
        outs.append(mx)
        left = n - 1 - it
        if left == 0:
            break
        first = jnp.min(jnp.where(cols[0] == mx, sub, SUB), axis=0, keepdims=True)
        pop = sub == first
        for lvl in range(min(groups, left)):
            below = cols[lvl + 1] if lvl + 1 < groups else NEG_INF
            cols[lvl] = jnp.where(pop, below, cols[lvl])
    return outs


def _twin_bf16_words(x):
    bits = pltpu.bitcast(x.astype(BF16).astype(F32), jnp.uint32)
    return bits | (bits >> 16)


def _route_kernel(x_ref, mod_ref, wf_ref, h2_ref, e2_ref, rk_ref, c1_ref, n1_ref):
    hd = pl.program_id(1)
    m = mod_ref[0]
    shift, scale = m[3:4], m[4:5]

    @pl.when(hd == 0)
    def _():
        h2_ref[...] = (x_ref[...] * (1.0 + scale) + shift).T.astype(BF16)

    h2 = h2_ref[...]
    s1 = _dot(wf_ref[0:N_KEYS, :], h2)
    s2 = _dot(wf_ref[N_KEYS:2 * N_KEYS, :], h2)
    ta = _top_sorted(s1, N_TOP)
    tb = _top_sorted(s2, N_TOP)
    rows = [ta[a] + tb[b] for a, b in CAND]
    rows += [jnp.full_like(rows[0], NEG_INF)] * (CAND_ROWS - len(CAND))
    cand = jnp.concatenate(rows, axis=0)
    tc = _top_sorted(cand, N_TOP)
    tau = 0.5 * (tc[PEER_TOPK - 1] + tc[PEER_TOPK])
    top = ta[0] + tb[0]
    zsum = jnp.sum(jnp.where(cand >= tau, jnp.exp(cand - top), 0.0), axis=0, keepdims=True)
    need = tau - s1
    rank2 = jnp.zeros_like(s2)
    count1 = jnp.zeros_like(s1)
    for b in range(PEER_TOPK):
        rank2 = rank2 + jnp.where(tb[b] > s2, 1.0, 0.0)
        count1 = count1 + jnp.where(tb[b] >= need, 1.0, 0.0)
    e2_ref[0] = jnp.exp(s2 - tb[0]).astype(BF16)
    rk_ref[0] = rank2.astype(BF16)
    c1w = _twin_bf16_words(jnp.exp(s1 - ta[0]) / zsum)
    n1w = _twin_bf16_words(count1)
    for tb in range(c1_ref.shape[1]):
        c1_ref[0, tb] = c1w[:, tb * LANES:(tb + 1) * LANES]
        n1_ref[0, tb] = n1w[:, tb * LANES:(tb + 1) * LANES]


def _peer_route(x2d, mod_l, wf, tt, tiles_per_batch):
    t, d = x2d.shape
    nh = PEER_HEADS
    per_head = pl.BlockSpec((1, N_KEYS, tt), lambda i, h: (h, 0, i))
    per_key = pl.BlockSpec((1, tt // LANES, N_KEYS, LANES), lambda i, h: (h, i, 0, 0))
    return pl.pallas_call(
        _route_kernel,
        name="peer_route",
        grid=(t // tt, nh),
        in_specs=[pl.BlockSpec((tt, d), lambda i, h: (i, 0)),
                  pl.BlockSpec((1, 6, d), lambda i, h: (i // tiles_per_batch, 0, 0)),
                  pl.BlockSpec((2 * N_KEYS, d), lambda i, h: (h, 0))],
        out_specs=[pl.BlockSpec((d, tt), lambda i, h: (0, i)), per_head, per_head, per_key, per_key],
        out_shape=[jax.ShapeDtypeStruct((d, t), BF16),
                   jax.ShapeDtypeStruct((nh, N_KEYS, t), BF16),
                   jax.ShapeDtypeStruct((nh, N_KEYS, t), BF16),
                   jax.ShapeDtypeStruct((nh, t // LANES, N_KEYS, LANES), jnp.uint32),
                   jax.ShapeDtypeStruct((nh, t // LANES, N_KEYS, LANES), jnp.uint32)],
        compiler_params=_params(("arbitrary", "arbitrary")),
    )(x2d, mod_l, wf)


ROWS = 16
TOKEN_COLS = 256
MXU_ROWS = 256
MXU_COLS = 256


def _gate_kernel(h2_ref, e2_ref, rk_ref, c1_ref, n1_ref, u_ref, p_ref, za_ref, zb_ref, *, eb):
    e = pl.program_id(1)
    tt = h2_ref.shape[1]
    slabs = eb // N_KEYS

    @pl.when(e == 0)
    def _():
        zb_ref[...] = jnp.zeros_like(zb_ref)

    def gate_blocks(z_ref):
        blocks = []
        for il in range(slabs):
            i1 = jnp.clip((e - 1) * slabs + il, 0, N_KEYS - 1)
            for t0 in range(0, tt, TOKEN_COLS):
                cols = slice(t0, t0 + TOKEN_COLS)
                rows_of = {}

                def token_rows(i1=i1, t0=t0, rows_of=rows_of):
                    if not rows_of:
                        def tile_of(ref, hd):
                            words = [jnp.broadcast_to(ref[hd, tb, pl.ds(i1, 1), :], (ROWS // 2, LANES))
                                     for tb in range(t0 // LANES, (t0 + TOKEN_COLS) // LANES)]
                            return pltpu.bitcast(jnp.concatenate(words, axis=1), BF16)
                        rows_of["cnt"] = [tile_of(n1_ref, hd) for hd in range(PEER_HEADS)]
                        rows_of["wgt"] = [tile_of(c1_ref, hd) for hd in range(PEER_HEADS)]
                    return rows_of["cnt"], rows_of["wgt"]

                for jb in range(N_KEYS // ROWS):
                    def block(il=il, jb=jb, cols=cols, token_rows=token_rows):
                        cnt, wgt = token_rows()
                        r0 = il * N_KEYS + jb * ROWS
                        zz = z_ref[r0:r0 + ROWS, cols].astype(BF16)
                        act = 0.5 * zz * (1.0 + lax.erf(zz * INV_SQRT2))
                        gsum = jnp.zeros((ROWS, TOKEN_COLS), BF16)
                        for hd in range(PEER_HEADS):
                            keep = rk_ref[hd, jb * ROWS:(jb + 1) * ROWS, cols] < cnt[hd]
                            gsum = gsum + jnp.where(keep, e2_ref[hd, jb * ROWS:(jb + 1) * ROWS, cols], 0.0) * wgt[hd]
                        p_ref[r0:r0 + ROWS, cols] = act * gsum
                    blocks.append(block)
        return blocks

    def matmul_units(z_new):
        units = []
        for n0 in range(0, tt, MXU_COLS):
            for m0 in range(0, eb, MXU_ROWS):
                def unit(m0=m0, n0=n0):
                    z_new[m0:m0 + MXU_ROWS, n0:n0 + MXU_COLS] = _dot(u_ref[m0:m0 + MXU_ROWS, :], h2_ref[:, n0:n0 + MXU_COLS])
                units.append(unit)
        return units

    def step(z_new, z_old):
        blocks = gate_blocks(z_old)
        units = matmul_units(z_new)
        per_unit = -(-len(blocks) // len(units))
        for i, unit in enumerate(units):
            unit()
            for block in blocks[i * per_unit:(i + 1) * per_unit]:
                block()

    @pl.when(e % 2 == 0)
    def _():
        step(za_ref, zb_ref)

    @pl.when(e % 2 == 1)
    def _():
        step(zb_ref, za_ref)


def _peer_gates(h2, e2, rk2, c1, n1, u_bf, tt, eb):
    d, t = h2.shape
    ne = u_bf.shape[0]
    nh = PEER_HEADS
    nslab = ne // eb
    per_tile = pl.BlockSpec((nh, N_KEYS, tt), lambda i, e: (0, 0, i))
    per_key = pl.BlockSpec((nh, tt // LANES, N_KEYS, LANES), lambda i, e: (0, i, 0, 0))
    return pl.pallas_call(
        functools.partial(_gate_kernel, eb=eb),
        name="peer_gates",
        grid=(t // tt, nslab + 1),
        in_specs=[pl.BlockSpec((d, tt), lambda i, e: (0, i)),
                  per_tile, per_tile, per_key, per_key,
                  pl.BlockSpec((eb, d), lambda i, e: (jnp.minimum(e, nslab - 1), 0))],
        out_specs=pl.BlockSpec((eb, tt), lambda i, e: (jnp.maximum(e - 1, 0), i)),
        out_shape=jax.ShapeDtypeStruct((ne, t), BF16),
        scratch_shapes=[pltpu.VMEM((eb, tt), F32),
                        pltpu.VMEM((eb, tt), F32)],
        compiler_params=_params(("arbitrary", "arbitrary")),
    )(h2, e2, rk2, c1, n1, u_bf)


def _out_kernel(vt_ref, p_ref, x_ref, mod_ref, ln_ref, o_ref, acc_ref):
    k = pl.program_id(1)

    @pl.when(k == 0)
    def _():
        acc_ref[...] = jnp.zeros_like(acc_ref)

    acc_ref[...] += _dot(vt_ref[0], p_ref[...])

    @pl.when(k == pl.num_programs(1) - 1)
    def _():
        m = mod_ref[0]
        gate = m[5:6]
        ln = ln_ref[...]
        y = acc_ref[...].T
        o_ref[...] = _layer_norm_rows(ALPHA * x_ref[...] + gate * y, ln[0:1], ln[1:2])


def _peer_out(p, vt_blocks, x2d, mod_l, ln_g, ln_b, tn, tiles_per_batch):
    t, d = x2d.shape
    nkb, _, kb = vt_blocks.shape
    ln = jnp.zeros((8, d), F32).at[0].set(ln_g).at[1].set(ln_b)
    return pl.pallas_call(
        _out_kernel,
        name="peer_out",
        grid=(t // tn, nkb),
        in_specs=[pl.BlockSpec((1, d, kb), lambda i, k: (k, 0, 0)),
                  pl.BlockSpec((kb, tn), lambda i, k: (k, i)),
                  pl.BlockSpec((tn, d), lambda i, k: (i, 0)),
                  pl.BlockSpec((1, 6, d), lambda i, k: (i // tiles_per_batch, 0, 0)),
                  pl.BlockSpec((8, d), lambda i, k: (0, 0))],
        out_specs=pl.BlockSpec((tn, d), lambda i, k: (i, 0)),
        out_shape=jax.ShapeDtypeStruct((t, d), F32),
        scratch_shapes=[pltpu.VMEM((d, tn), F32)],
        compiler_params=_params(("arbitrary", "arbitrary")),
    )(vt_blocks, p, x2d, mod_l, ln)


def _peer_ffn(x, mod_l, wf, u_tab, v_tab, ln_g, ln_b, tt, eb, tn, kb):
    b, s, d = x.shape
    x2d = x.reshape(b * s, d)
    h2, e2, rk2, c1, n1 = _peer_route(x2d, mod_l, wf, tt, s // tt)
    p = _peer_gates(h2, e2, rk2, c1, n1, u_tab.astype(BF16), tt, eb)
    vt = v_tab.astype(BF16).reshape(v_tab.shape[0] // kb, kb, d).transpose(0, 2, 1)
    out = _peer_out(p, vt, x2d, mod_l, ln_g, ln_b, tn, s // tn)
    return out.reshape(b, s, d)


def kernel(x, c, ada_w, ada_b, ln_g, ln_b, rw_mu, rw_w_rkv, rw_w0, rw_w1, rw_w2, rw_a0, rw_a1, rw_a2, rw_g1, rw_g2, rw_k_k, rw_k_a, rw_r_k, rw_lnx_g, rw_lnx_b, rw_w_o, pl_w_in, pl_w_grp, pl_scale, pl_w_out, pe_w_q, pe_keys, pe_u, pe_v):
    b, s, d = x.shape
    depth = ada_w.shape[0]
    ts = min(256, s)
    tc = min(1024, s)
    tt = min(512, s)
    eb = 512
    tn = min(1024, s)
    kb = 2048
    mod = _adaln_mod(c, ada_w, ada_b)
    wf = _peer_fold(pe_keys, pe_w_q)
    for i in range(depth):
        j = i // 2
        if i % 2 == 0:
            r, lw, k, v, kkr, a, g = _rwkv_proj(x, mod[i], rw_mu[j], rw_w_rkv[j], rw_w0[j], rw_w1[j], rw_w2[j],
                                                rw_a0[j], rw_a1[j], rw_a2[j], rw_g1[j], rw_g2[j],
                                                rw_k_k[j], rw_k_a[j], ts)
            y = _wkv_scan(r, lw, k, v, kkr, a, rw_r_k[j], rw_lnx_g[j], rw_lnx_b[j], tc)
            x = _mix_out(y, g, x, mod[i], rw_w_o[j], ln_g[i, 0], ln_b[i, 0], ts)
        else:
            x = _pool_mix(x, mod[i], pl_w_in[j], pl_w_grp[j], pl_scale[j], pl_w_out[j],
                          ln_g[i, 0], ln_b[i, 0], ts)
        x = _peer_ffn(x, mod[i], wf[i], pe_u[i], pe_v[i], ln_g[i, 1], ln_b[i, 1], tt, eb, tn, kb)
    return x
```

```python
import functools
import math

import jax
import jax.numpy as jnp
from jax import lax
from jax.experimental import pallas as pl
from jax.experimental.pallas import tpu as pltpu

F32 = jnp.float32
BF16 = jnp.bfloat16
HI = lax.Precision.HIGHEST

HEAD = 64
PAIR = 2 * HEAD
SUB = 8
LANES = 128
CHUNK = 64
GN_EPS = 64e-5
LN_EPS = 1e-5
DEPTH = 2
ALPHA = (2 * DEPTH) ** 0.25
POOL_WINDOWS = (2, 4, 8, 16)
POOL_HALO = 16
N_KEYS = 128
PEER_HEADS = 8
PEER_TOPK = 16
INV_SQRT2 = 0.7071067811865476
NEG_INF = float("-inf")
V7X_VMEM_LIMIT = 56 * 1024 * 1024

NT_DIMS = (((1,), (1,)), ((), ()))
TN_DIMS = (((0,), (0,)), ((), ()))


def _dot(a, b, precision=None):
    return jnp.dot(a, b, precision=precision, preferred_element_type=F32)


def _dot_nt(a, b):
    return lax.dot_general(a, b, NT_DIMS, preferred_element_type=F32)


def _dot_tn(a, b):
    return lax.dot_general(a, b, TN_DIMS, preferred_element_type=F32)


def _split_bf16(x):
    hi = x.astype(BF16)
    return hi, (x - hi.astype(F32)).astype(BF16)


def _dot_sel(x, sel):
    hi, lo = _split_bf16(x)
    sel = sel.astype(BF16)
    return _dot(hi, sel) + _dot(lo, sel)


def _sel_dot(sel, x):
    hi, lo = _split_bf16(x)
    sel = sel.astype(BF16)
    return _dot(sel, hi) + _dot(sel, lo)


def _params(sem, vmem=V7X_VMEM_LIMIT, flags=None):
    return pltpu.CompilerParams(dimension_semantics=sem, vmem_limit_bytes=vmem, flags=flags)


def _layer_norm_rows(z, g, b):
    mu = jnp.mean(z, axis=-1, keepdims=True)
    d = z - mu
    var = jnp.mean(d * d, axis=-1, keepdims=True)
    return d * lax.rsqrt(var + LN_EPS) * g + b


def _mod_kernel(c_ref, w_ref, b_ref, o_ref):
    c = c_ref[...]
    cond = c * jax.nn.sigmoid(c)
    o_ref[0] = _dot(cond, w_ref[0], HI) + b_ref[0]


def _adaln_mod(c, ada_w, ada_b):
    depth, d, nd = ada_w.shape
    b = c.shape[0]
    nmod = nd // d
    out = pl.pallas_call(
        _mod_kernel,
        name="adaln_mod",
        grid=(depth, nmod),
        in_specs=[
            pl.BlockSpec((b, d), lambda l, n: (0, 0)),
            pl.BlockSpec((1, d, d), lambda l, n: (l, 0, n)),
            pl.BlockSpec((1, 1, d), lambda l, n: (l, 0, n)),
        ],
        out_specs=pl.BlockSpec((1, b, d), lambda l, n: (l, 0, n)),
        out_shape=jax.ShapeDtypeStruct((depth, b, nd), F32),
        compiler_params=_params(("arbitrary", "arbitrary")),
    )(c, ada_w, ada_b.reshape(depth, 1, nd))
    return out.reshape(depth, b, nmod, d)


def _softplus(z):
    return jnp.maximum(z, 0.0) + jnp.log1p(jnp.exp(-jnp.abs(z)))


def _rwkv_proj_kernel(x_ref, xp_ref, mod_ref, mu_ref, vec_ref, wrkv_ref, w1_ref, w2_ref, a1_ref, a2_ref,
                      g1_ref, g2_ref, r_ref, lw_ref, k_ref, v_ref, kk_ref, a_ref, g_ref):
    s = pl.program_id(1)
    m = mod_ref[0]
    shift, scale = m[0:1], m[1:2]
    h = x_ref[0] * (1.0 + scale) + shift
    prev = xp_ref[0][7:8] * (1.0 + scale) + shift
    prev = jnp.where(s == 0, 0.0, prev)
    row = lax.broadcasted_iota(jnp.int32, h.shape, 0)
    hprev = jnp.where(row == 0, prev, pltpu.roll(h, 1, 0))
    xx = hprev - h
    mu = mu_ref[...]

    def mix(n):
        return (h + xx * mu[n:n + 1]).astype(BF16)

    vec = vec_ref[...]
    w0, a0, k_k, k_a = vec[0:1], vec[1:2], vec[2:3], vec[3:4]
    r = _dot(mix(0), wrkv_ref[0])
    k = _dot(mix(1), wrkv_ref[1])
    v = _dot(mix(2), wrkv_ref[2])
    wl = w0 + _dot(jnp.tanh(_dot(mix(3), w1_ref[...])).astype(BF16), w2_ref[...])
    w = -_softplus(-wl) - 0.5
    a = jax.nn.sigmoid(a0 + _dot(_dot(mix(4), a1_ref[...]).astype(BF16), a2_ref[...]))
    g = _dot(jax.nn.sigmoid(_dot(mix(5), g1_ref[...])).astype(BF16), g2_ref[...])
    r_ref[0] = r
    lw_ref[0] = -jnp.exp(w)
    k_ref[0] = k * (1.0 + (a - 1.0) * k_a)
    v_ref[0] = v
    kk_ref[0] = k * k_k
    a_ref[0] = a
    g_ref[0] = g.astype(BF16)


def _pad_cols(w, n):
    return jnp.pad(w, ((0, 0), (0, n - w.shape[1])))


def _pad_rows(w, n):
    return jnp.pad(w, ((0, n - w.shape[0]), (0, 0)))


def _rwkv_proj(x, mod_l, mu, w_rkv, w0, w1, w2, a0, a1, a2, g1, g2, k_k, k_a, ts):
    b, s, d = x.shape
    lora = 128
    glora = 256
    vec = jnp.zeros((8, d), F32).at[0].set(w0).at[1].set(a0).at[2].set(k_k).at[3].set(k_a)
    tile = pl.BlockSpec((1, ts, d), lambda i, j: (i, j, 0))
    full2 = lambda shape: pl.BlockSpec(shape, lambda i, j: (0, 0))
    outs = pl.pallas_call(
        _rwkv_proj_kernel,
        name="rwkv_proj",
        grid=(b, s // ts),
        in_specs=[
            tile,
            pl.BlockSpec((1, 8, d), lambda i, j: (i, jnp.maximum(j * (ts // 8) - 1, 0), 0)),
            pl.BlockSpec((1, 6, d), lambda i, j: (i, 0, 0)),
            full2((6, d)),
            full2((8, d)),
            pl.BlockSpec((3, d, d), lambda i, j: (0, 0, 0)),
            full2((d, lora)), full2((lora, d)),
            full2((d, lora)), full2((lora, d)),
            full2((d, glora)), full2((glora, d)),
        ],
        out_specs=[tile] * 7,
        out_shape=[jax.ShapeDtypeStruct((b, s, d), F32)] * 6 + [jax.ShapeDtypeStruct((b, s, d), BF16)],
        compiler_params=_params(("arbitrary", "arbitrary")),
    )(x, x, mod_l, mu, vec, w_rkv.astype(BF16),
      _pad_cols(w1, lora).astype(BF16), _pad_rows(w2, lora).astype(BF16),
      _pad_cols(a1, lora).astype(BF16), _pad_rows(a2, lora).astype(BF16),
      _pad_cols(g1, glora).astype(BF16), _pad_rows(g2, glora).astype(BF16))
    return outs


def _head_masks(shape):
    lane = lax.broadcasted_iota(jnp.int32, shape, len(shape) - 1)
    first = (lane % PAIR) < HEAD
    return first, jnp.logical_not(first)


def _stack_heads(z):
    m0, m1 = _head_masks(z.shape)
    return jnp.concatenate([jnp.where(m0, z, 0.0), jnp.where(m1, z, 0.0)], axis=0)


def _wkv_kernel(r_ref, lw_ref, k_ref, v_ref, kk_ref, a_ref, rk_ref, lng_ref, lnb_ref, y_ref,
                s_ref, yb_ref, *, nchunk):
    L = CHUNK

    @pl.when(pl.program_id(2) == 0)
    def _():
        s_ref[...] = jnp.zeros_like(s_ref)

    r2 = lax.broadcasted_iota(jnp.int32, (PAIR, PAIR), 0)
    c2 = lax.broadcasted_iota(jnp.int32, (PAIR, PAIR), 1)
    same_head = (r2 // HEAD) == (c2 // HEAD)
    eye = r2 == c2
    bd_ones = jnp.where(same_head, 1.0, 0.0).astype(F32)
    tr = lax.broadcasted_iota(jnp.int32, (L, PAIR), 0)
    tc = lax.broadcasted_iota(jnp.int32, (L, PAIR), 1) % HEAD
    strict = tr > tc
    incl = tr >= tc

    def off_diag(size):
        same = (tr // (2 * size)) == (tc // (2 * size))
        return same & ((tr % (2 * size)) >= size) & ((tc % (2 * size)) < size)
    lr = lax.broadcasted_iota(jnp.int32, (L, L), 0)
    lc = lax.broadcasted_iota(jnp.int32, (L, L), 1)
    ltri = jnp.where(lr >= lc, 1.0, 0.0).astype(F32)

    chunks = range(nchunk)
    r_all = r_ref[0]
    lw_all = lw_ref[0]
    k_all = k_ref[0]
    v_all = v_ref[0]
    a_all = a_ref[0]
    kkr = kk_ref[0]
    kk = kkr * lax.rsqrt(jnp.maximum(_dot_sel(kkr * kkr, bd_ones), 1e-24))
    bv_all = kk * a_all

    def rows(x, c):
        return x[c * L:(c + 1) * L]

    cs = [_sel_dot(ltri, rows(lw_all, c)) for c in chunks]
    rt, at, vst, lhs, rhs, bc, kc, p_last = [], [], [], [], [], [], [], []
    for c in chunks:
        cs_last = cs[c][L - 1:L, :]
        pinv = jnp.exp(-cs[c])
        prem = jnp.exp(cs_last - cs[c])
        rt.append(rows(r_all, c) * jnp.exp(cs[c]))
        at.append(-rows(kk, c) * jnp.exp(cs[c] - rows(lw_all, c)))
        bt = rows(bv_all, c) * pinv
        kt = rows(k_all, c) * pinv
        bc.append((rows(bv_all, c) * prem).astype(BF16))
        kc.append((rows(k_all, c) * prem).astype(BF16))
        p_last.append(jnp.exp(cs_last))
        vst.append(_stack_heads(rows(v_all, c)).astype(BF16))
        lhs.append(jnp.concatenate([at[c], rt[c]], axis=0).astype(BF16))
        rhs.append(jnp.concatenate([_stack_heads(bt), _stack_heads(kt)], axis=0).astype(BF16))
    o = [_dot_nt(lhs[c], rhs[c]) for c in chunks]
    nmat = [jnp.where(strict, o[c][:L, :PAIR], 0.0) for c in chunks]
    akv = [_dot(jnp.where(strict, o[c][:L, PAIR:], 0.0).astype(BF16), vst[c]) for c in chunks]
    ident = jnp.where(tr == tc, 1.0, 0.0)
    tinv = [ident + jnp.where(off_diag(1), nmat[c], 0.0) for c in chunks]
    size = 2
    while size < L:
        mask = off_diag(size)
        tn = [_dot(tinv[c].astype(BF16), _stack_heads(jnp.where(mask, nmat[c], 0.0)).astype(BF16)) for c in chunks]
        tinv = [tinv[c] + _dot(tn[c].astype(BF16), _stack_heads(tinv[c]).astype(BF16)) for c in chunks]
        size *= 2
    z = [_dot(tinv[c].astype(BF16), _stack_heads(jnp.concatenate([at[c], akv[c]], axis=1)).astype(BF16))
         for c in chunks]
    tmp = [_dot(jnp.where(incl, o[c][L:, :PAIR], 0.0).astype(BF16), _stack_heads(z[c]).astype(BF16))
           for c in chunks]
    rkv = [_dot(jnp.where(incl, o[c][L:, PAIR:], 0.0).astype(BF16), vst[c]) for c in chunks]
    bz = [_dot_tn(bc[c], z[c].astype(BF16)) for c in chunks]
    kv = [_dot_tn(kc[c], rows(v_all, c).astype(BF16)) for c in chunks]
    state = s_ref[...]
    for c in chunks:
        sb = state.astype(BF16)
        rh = (rt[c] + tmp[c][:, :PAIR]).astype(BF16)
        yb_ref[c * L:(c + 1) * L, :] = _dot(rh, sb) + tmp[c][:, PAIR:] + rkv[c]
        m_c = jnp.where(eye, jnp.broadcast_to(p_last[c], (PAIR, PAIR)), 0.0) + jnp.where(same_head, bz[c][:, :PAIR], 0.0)
        state = _dot(m_c.astype(BF16), sb) + jnp.where(same_head, bz[c][:, PAIR:] + kv[c], 0.0)
    s_ref[...] = state

    y = yb_ref[...]
    bd_avg = bd_ones * (1.0 / HEAD)
    ym = _dot_sel(y, bd_avg)
    d = y - ym
    yv = _dot_sel(d * d, bd_avg)
    yn = d * lax.rsqrt(yv + GN_EPS) * lng_ref[...] + lnb_ref[...]
    bonus = _dot_sel(r_all * k_all * rk_ref[...], bd_ones) * v_all
    y_ref[0] = yn + bonus


def _wkv_scan(r, lw, k, v, kkr, a, r_k, lnx_g, lnx_b, tc):
    b, s, d = r.shape
    nchunk = tc // CHUNK
    tile = pl.BlockSpec((1, tc, PAIR), lambda i, p, j: (i, j, p))
    row = pl.BlockSpec((1, PAIR), lambda i, p, j: (0, p))
    return pl.pallas_call(
        functools.partial(_wkv_kernel, nchunk=nchunk),
        name="wkv_scan",
        grid=(b, d // PAIR, s // tc),
        in_specs=[tile] * 6 + [row] * 3,
        out_specs=tile,
        out_shape=jax.ShapeDtypeStruct((b, s, d), F32),
        scratch_shapes=[
            pltpu.VMEM((PAIR, PAIR), F32),
            pltpu.VMEM((tc, PAIR), F32),
        ],
        compiler_params=_params(("arbitrary", "arbitrary", "arbitrary")),
    )(r, lw, k, v, kkr, a, r_k.reshape(1, d), lnx_g.reshape(1, d), lnx_b.reshape(1, d))


def _mix_out_kernel(y_ref, g_ref, x_ref, mod_ref, wo_ref, ln_ref, o_ref):
    m = mod_ref[0]
    gate = m[2:3]
    yg = (y_ref[0] * g_ref[0].astype(F32)).astype(BF16)
    o = _dot(yg, wo_ref[...])
    z = ALPHA * x_ref[0] + gate * o
    ln = ln_ref[...]
    o_ref[0] = _layer_norm_rows(z, ln[0:1], ln[1:2])


def _mix_out(y, g, x, mod_l, w_o, ln_g, ln_b, ts):
    b, s, d = x.shape
    tile = pl.BlockSpec((1, ts, d), lambda i, j: (i, j, 0))
    ln = jnp.zeros((8, d), F32).at[0].set(ln_g).at[1].set(ln_b)
    return pl.pallas_call(
        _mix_out_kernel,
        name="mix_out",
        grid=(b, s // ts),
        in_specs=[tile, tile, tile,
                  pl.BlockSpec((1, 6, d), lambda i, j: (i, 0, 0)),
                  pl.BlockSpec((d, d), lambda i, j: (0, 0)),
                  pl.BlockSpec((8, d), lambda i, j: (0, 0))],
        out_specs=tile,
        out_shape=jax.ShapeDtypeStruct((b, s, d), F32),
        compiler_params=_params(("arbitrary", "arbitrary")),
    )(y, g, x, mod_l, w_o.astype(BF16), ln)


def _pool_kernel(x_ref, mod_ref, win_ref, wgrp_ref, sc_ref, wout_ref, ln_ref, o_ref, zext_ref, *, ts):
    s = pl.program_id(1)
    m = mod_ref[0]
    shift, scale, gate = m[0:1], m[1:2], m[2:3]
    x = x_ref[0]
    h = x * (1.0 + scale) + shift
    z = _dot(h.astype(BF16), win_ref[...])

    @pl.when(s == 0)
    def _():
        zext_ref[0:POOL_HALO, :] = jnp.zeros((POOL_HALO, z.shape[1]), F32)

    zext_ref[POOL_HALO:POOL_HALO + ts, :] = z
    pos = s * ts + lax.broadcasted_iota(jnp.int32, (ts, 1), 0)
    gw = z.shape[1] // len(POOL_WINDOWS)
    parts = []
    for gi, win in enumerate(POOL_WINDOWS):
        lo = gi * gw
        zg = z[:, lo:lo + gw]
        acc = zg
        for back in range(1, win):
            acc = acc + zext_ref[POOL_HALO - back:POOL_HALO - back + ts, lo:lo + gw]
        cnt = jnp.minimum(pos + 1, win).astype(F32)
        p = acc / cnt - zg
        parts.append(_dot(p.astype(BF16), wgrp_ref[gi]))
    y = jnp.concatenate(parts, axis=1) * sc_ref[...]
    o = _dot(y.astype(BF16), wout_ref[...])
    zext_ref[0:POOL_HALO, :] = zext_ref[ts:ts + POOL_HALO, :]
    ln = ln_ref[...]
    o_ref[0] = _layer_norm_rows(ALPHA * x + gate * o, ln[0:1], ln[1:2])


def _pool_mix(x, mod_l, w_in, w_grp, scale, w_out, ln_g, ln_b, ts):
    b, s, d = x.shape
    ng, gw, _ = w_grp.shape
    tile = pl.BlockSpec((1, ts, d), lambda i, j: (i, j, 0))
    ln = jnp.zeros((8, d), F32).at[0].set(ln_g).at[1].set(ln_b)
    return pl.pallas_call(
        functools.partial(_pool_kernel, ts=ts),
        name="pool_mix",
        grid=(b, s // ts),
        in_specs=[tile,
                  pl.BlockSpec((1, 6, d), lambda i, j: (i, 0, 0)),
                  pl.BlockSpec((d, d), lambda i, j: (0, 0)),
                  pl.BlockSpec((ng, gw, gw), lambda i, j: (0, 0, 0)),
                  pl.BlockSpec((1, d), lambda i, j: (0, 0)),
                  pl.BlockSpec((d, d), lambda i, j: (0, 0)),
                  pl.BlockSpec((8, d), lambda i, j: (0, 0))],
        out_specs=tile,
        out_shape=jax.ShapeDtypeStruct((b, s, d), F32),
        scratch_shapes=[pltpu.VMEM((ts + POOL_HALO, d), F32)],
        compiler_params=_params(("arbitrary", "arbitrary")),
    )(x, mod_l, w_in.astype(BF16), w_grp.astype(BF16), scale.reshape(1, d), w_out.astype(BF16), ln)


def _fold_kernel(keys_ref, wq_ref, o_ref):
    o_ref[0] = lax.dot_general(keys_ref[0, 0], wq_ref[0], NT_DIMS, precision=HI,
                               preferred_element_type=F32).astype(BF16)


def _peer_fold(keys, w_q):
    depth, nh, two, nk, dk = keys.shape
    d = w_q.shape[1]
    nb = nh * two
    return pl.pallas_call(
        _fold_kernel,
        name="peer_fold",
        grid=(depth, nb),
        in_specs=[pl.BlockSpec((1, 1, nk, dk), lambda l, j: (l, j, 0, 0)),
                  pl.BlockSpec((1, d, dk), lambda l, j: (l, 0, j))],
        out_specs=pl.BlockSpec((1, nk, d), lambda l, j: (l, j, 0)),
        out_shape=jax.ShapeDtypeStruct((depth, nb * nk, d), BF16),
        compiler_params=_params(("arbitrary", "arbitrary")),
    )(keys.reshape(depth, nb, nk, dk), w_q)


N_TOP = PEER_TOPK + 1
CAND = [(a, b) for a in range(N_TOP) for b in range(N_TOP) if (a + 1) * (b + 1) <= N_TOP]
CAND_ROWS = 1 << (len(CAND) - 1).bit_length()


def _sorting_network(n):
    pairs = []

    def merge(lo, cnt, step):
        nxt = step * 2
        if nxt < cnt:
            merge(lo, cnt, nxt)
            merge(lo + step, cnt, nxt)
            for i in range(lo + step, lo + cnt - step, nxt):
                pairs.append((i, i + step))
        else:
            pairs.append((lo, lo + step))

    def sort(lo, cnt):
        if cnt > 1:
            half = cnt // 2
            sort(lo, half)
            sort(lo + half, half)
            merge(lo, cnt, 1)

    sort(0, n)
    return pairs


def _top_sorted(s, n):
    groups = s.shape[0] // SUB
    cols = [s[g * SUB:(g + 1) * SUB, :] for g in range(groups)]
    for i, j in _sorting_network(groups):
        cols[i], cols[j] = jnp.maximum(cols[i], cols[j]), jnp.minimum(cols[i], cols[j])
    sub = lax.broadcasted_iota(jnp.int32, cols[0].shape, 0)
    outs = []
    for it in range(n):
        mx = jnp.max(cols[0], axis=0, keepdims=True)
        outs.append(mx)
        left = n - 1 - it
        if left == 0:
            break
        first = jnp.min(jnp.where(cols[0] == mx, sub, SUB), axis=0, keepdims=True)
        pop = sub == first
        for lvl in range(min(groups, left)):
            below = cols[lvl + 1] if lvl + 1 < groups else NEG_INF
            cols[lvl] = jnp.where(pop, below, cols[lvl])
    return outs


def _twin_bf16_words(x):
    bits = pltpu.bitcast(x.astype(BF16).astype(F32), jnp.uint32)
    return bits | (bits >> 16)


def _route_kernel(x_ref, mod_ref, wf_ref, h2_ref, e2_ref, rk_ref, c1_ref, n1_ref):
    hd = pl.program_id(1)
    m = mod_ref[0]
    shift, scale = m[3:4], m[4:5]

    @pl.when(hd == 0)
    def _():
        h2_ref[...] = (x_ref[...] * (1.0 + scale) + shift).T.astype(BF16)

    h2 = h2_ref[...]
    s1 = _dot(wf_ref[0:N_KEYS, :], h2)
    s2 = _dot(wf_ref[N_KEYS:2 * N_KEYS, :], h2)
    ta = _top_sorted(s1, N_TOP)
    tb = _top_sorted(s2, N_TOP)
    rows = [ta[a] + tb[b] for a, b in CAND]
    rows += [jnp.full_like(rows[0], NEG_INF)] * (CAND_ROWS - len(CAND))
    cand = jnp.concatenate(rows, axis=0)
    tc = _top_sorted(cand, N_TOP)
    tau = 0.5 * (tc[PEER_TOPK - 1] + tc[PEER_TOPK])
    top = ta[0] + tb[0]
    zsum = jnp.sum(jnp.where(cand >= tau, jnp.exp(cand - top), 0.0), axis=0, keepdims=True)
    need = tau - s1
    rank2 = jnp.zeros_like(s2)
    count1 = jnp.zeros_like(s1)
    for b in range(PEER_TOPK):
        rank2 = jnp.where(tb[b] > s2, b + 1.0, rank2)
        count1 = jnp.where(tb[b] >= need, b + 1.0, count1)
    e2_ref[0] = jnp.exp(s2 - tb[0]).astype(BF16)
    rk_ref[0] = rank2.astype(BF16)
    c1w = _twin_bf16_words(jnp.exp(s1 - ta[0]) / zsum)
    n1w = _twin_bf16_words(count1)
    for blk in range(c1_ref.shape[1]):
        c1_ref[0, blk] = c1w[:, blk * LANES:(blk + 1) * LANES]
        n1_ref[0, blk] = n1w[:, blk * LANES:(blk + 1) * LANES]


def _peer_route(x2d, mod_l, wf, tt, tiles_per_batch):
    t, d = x2d.shape
    nh = PEER_HEADS
    per_head = pl.BlockSpec((1, N_KEYS, tt), lambda i, h: (h, 0, i))
    per_key = pl.BlockSpec((1, tt // LANES, N_KEYS, LANES), lambda i, h: (h, i, 0, 0))
    return pl.pallas_call(
        _route_kernel,
        name="peer_route",
        grid=(t // tt, nh),
        in_specs=[pl.BlockSpec((tt, d), lambda i, h: (i, 0)),
                  pl.BlockSpec((1, 6, d), lambda i, h: (i // tiles_per_batch, 0, 0)),
                  pl.BlockSpec((2 * N_KEYS, d), lambda i, h: (h, 0))],
        out_specs=[pl.BlockSpec((d, tt), lambda i, h: (0, i)), per_head, per_head, per_key, per_key],
        out_shape=[jax.ShapeDtypeStruct((d, t), BF16),
                   jax.ShapeDtypeStruct((nh, N_KEYS, t), BF16),
                   jax.ShapeDtypeStruct((nh, N_KEYS, t), BF16),
                   jax.ShapeDtypeStruct((nh, t // LANES, N_KEYS, LANES), jnp.uint32),
                   jax.ShapeDtypeStruct((nh, t // LANES, N_KEYS, LANES), jnp.uint32)],
        compiler_params=_params(("arbitrary", "arbitrary")),
    )(x2d, mod_l, wf)


ROWS = 16
TOKEN_COLS = 256
MXU_ROWS = 256
MXU_COLS = 256


def _dense_kernel(h2_ref, e2_ref, rk_ref, c1_ref, n1_ref, u_ref, vt_ref, x_ref, mod_ref, ln_ref, o_ref,
                  acc_ref, za_ref, zb_ref, pa_ref, pb_ref, *, eb):
    e = pl.program_id(1)
    last = pl.num_programs(1) - 1
    tt = h2_ref.shape[1]
    slabs = eb // N_KEYS

    @pl.when(e == 0)
    def _():
        acc_ref[...] = jnp.zeros_like(acc_ref)
        zb_ref[...] = jnp.zeros_like(zb_ref)
        pa_ref[...] = jnp.zeros_like(pa_ref)

    def gate_blocks(z_ref, p_ref):
        blocks = []
        for il in range(slabs):
            i1 = jnp.clip((e - 1) * slabs + il, 0, N_KEYS - 1)
            for t0 in range(0, tt, TOKEN_COLS):
                cols = slice(t0, t0 + TOKEN_COLS)
                rows_of = {}

                def token_rows(i1=i1, t0=t0, rows_of=rows_of):
                    if not rows_of:
                        def tile_of(ref, hd):
                            words = [jnp.broadcast_to(ref[hd, tb, pl.ds(i1, 1), :], (ROWS // 2, LANES))
                                     for tb in range(t0 // LANES, (t0 + TOKEN_COLS) // LANES)]
                            return pltpu.bitcast(jnp.concatenate(words, axis=1), BF16)
                        rows_of["cnt"] = [tile_of(n1_ref, hd) for hd in range(PEER_HEADS)]
                        rows_of["wgt"] = [tile_of(c1_ref, hd) for hd in range(PEER_HEADS)]
                    return rows_of["cnt"], rows_of["wgt"]

                for jb in range(N_KEYS // ROWS):
                    def block(il=il, jb=jb, cols=cols, token_rows=token_rows):
                        cnt, wgt = token_rows()
                        r0 = il * N_KEYS + jb * ROWS
                        zz = z_ref[r0:r0 + ROWS, cols].astype(BF16)
                        act = 0.5 * zz * (1.0 + lax.erf(zz * INV_SQRT2))
                        gsum = jnp.zeros((ROWS, TOKEN_COLS), BF16)
                        for hd in range(PEER_HEADS):
                            keep = rk_ref[hd, jb * ROWS:(jb + 1) * ROWS, cols] < cnt[hd]
                            gsum = gsum + jnp.where(keep, e2_ref[hd, jb * ROWS:(jb + 1) * ROWS, cols], 0.0) * wgt[hd]
                        p_ref[r0:r0 + ROWS, cols] = act * gsum
                    blocks.append(block)
        return blocks

    def matmul_units(z_new, p_old):
        units = []
        for n0 in range(0, tt, MXU_COLS):
            for m0 in range(0, acc_ref.shape[0], MXU_ROWS):
                def unit(m0=m0, n0=n0):
                    acc_ref[m0:m0 + MXU_ROWS, n0:n0 + MXU_COLS] += _dot(vt_ref[0, m0:m0 + MXU_ROWS, :], p_old[:, n0:n0 + MXU_COLS])
                units.append((eb, unit))
            for m0 in range(0, eb, MXU_ROWS):
                def unit(m0=m0, n0=n0):
                    z_new[m0:m0 + MXU_ROWS, n0:n0 + MXU_COLS] = _dot(u_ref[m0:m0 + MXU_ROWS, :], h2_ref[:, n0:n0 + MXU_COLS])
                units.append((u_ref.shape[1], unit))
        return units

    def step(z_new, z_old, p_new, p_old):
        blocks = gate_blocks(z_old, p_new)
        units = matmul_units(z_new, p_old)
        total = sum(k for k, _ in units)
        done = 0
        issued = 0
        for k, unit in units:
            unit()
            done += k
            upto = -(-len(blocks) * done // total)
            for block in blocks[issued:upto]:
                block()
            issued = upto

    @pl.when(e % 2 == 0)
    def _():
        step(za_ref, zb_ref, pb_ref, pa_ref)

    @pl.when(e % 2 == 1)
    def _():
        step(zb_ref, za_ref, pa_ref, pb_ref)

    @pl.when(e == last)
    def _():
        m = mod_ref[0]
        gate = m[5:6]
        ln = ln_ref[...]
        y = acc_ref[...].T
        o_ref[...] = _layer_norm_rows(ALPHA * x_ref[...] + gate * y, ln[0:1], ln[1:2])


def _peer_dense(h2, e2, rk2, c1, n1, u_bf, vt_bf, x2d, mod_l, ln_g, ln_b, tt, eb, tiles_per_batch):
    t, d = x2d.shape
    nh = PEER_HEADS
    ln = jnp.zeros((8, d), F32).at[0].set(ln_g).at[1].set(ln_b)
    nslab = u_bf.shape[0] // eb
    per_tile = pl.BlockSpec((nh, N_KEYS, tt), lambda i, e: (0, 0, i))
    per_key = pl.BlockSpec((nh, tt // LANES, N_KEYS, LANES), lambda i, e: (0, i, 0, 0))
    return pl.pallas_call(
        functools.partial(_dense_kernel, eb=eb),
        name="peer_dense",
        grid=(t // tt, nslab + 2),
        in_specs=[pl.BlockSpec((d, tt), lambda i, e: (0, i)),
                  per_tile, per_tile, per_key, per_key,
                  pl.BlockSpec((eb, d), lambda i, e: (jnp.minimum(e, nslab - 1), 0)),
                  pl.BlockSpec((1, d, eb), lambda i, e: (jnp.clip(e - 2, 0, nslab - 1), 0, 0)),
                  pl.BlockSpec((tt, d), lambda i, e: (i, 0)),
                  pl.BlockSpec((1, 6, d), lambda i, e: (i // tiles_per_batch, 0, 0)),
                  pl.BlockSpec((8, d), lambda i, e: (0, 0))],
        out_specs=pl.BlockSpec((tt, d), lambda i, e: (i, 0)),
        out_shape=jax.ShapeDtypeStruct((t, d), F32),
        scratch_shapes=[pltpu.VMEM((d, tt), F32),
                        pltpu.VMEM((eb, tt), F32),
                        pltpu.VMEM((eb, tt), F32),
                        pltpu.VMEM((eb, tt), BF16),
                        pltpu.VMEM((eb, tt), BF16)],
        compiler_params=_params(("arbitrary", "arbitrary")),
    )(h2, e2, rk2, c1, n1, u_bf, vt_bf, x2d, mod_l, ln)


def _peer_ffn(x, mod_l, wf, u_tab, v_tab, ln_g, ln_b, tt_route, tt_dense, eb):
    b, s, d = x.shape
    x2d = x.reshape(b * s, d)
    h2, e2, rk2, c1, n1 = _peer_route(x2d, mod_l, wf, tt_route, s // tt_route)
    vt = v_tab.astype(BF16).reshape(v_tab.shape[0] // eb, eb, d).transpose(0, 2, 1)
    out = _peer_dense(h2, e2, rk2, c1, n1, u_tab.astype(BF16), vt, x2d, mod_l, ln_g, ln_b,
                      tt_dense, eb, s // tt_dense)
    return out.reshape(b, s, d)


def kernel(x, c, ada_w, ada_b, ln_g, ln_b, rw_mu, rw_w_rkv, rw_w0, rw_w1, rw_w2, rw_a0, rw_a1, rw_a2, rw_g1, rw_g2, rw_k_k, rw_k_a, rw_r_k, rw_lnx_g, rw_lnx_b, rw_w_o, pl_w_in, pl_w_grp, pl_scale, pl_w_out, pe_w_q, pe_keys, pe_u, pe_v):
    b, s, d = x.shape
    depth = ada_w.shape[0]
    ts = min(256, s)
    tc = min(1024, s)
    tt = min(512, s)
    tt_dense = min(512, s)
    eb = 512
    mod = _adaln_mod(c, ada_w, ada_b)
    wf = _peer_fold(pe_keys, pe_w_q)
    for i in range(depth):
        j = i // 2
        if i % 2 == 0:
            r, lw, k, v, kkr, a, g = _rwkv_proj(x, mod[i], rw_mu[j], rw_w_rkv[j], rw_w0[j], rw_w1[j], rw_w2[j],
                                                rw_a0[j], rw_a1[j], rw_a2[j], rw_g1[j], rw_g2[j],
                                                rw_k_k[j], rw_k_a[j], ts)
            y = _wkv_scan(r, lw, k, v, kkr, a, rw_r_k[j], rw_lnx_g[j], rw_lnx_b[j], tc)
            x = _mix_out(y, g, x, mod[i], rw_w_o[j], ln_g[i, 0], ln_b[i, 0], ts)
        else:
            x = _pool_mix(x, mod[i], pl_w_in[j], pl_w_grp[j], pl_scale[j], pl_w_out[j],
                          ln_g[i, 0], ln_b[i, 0], ts)
        x = _peer_ffn(x, mod[i], wf[i], pe_u[i], pe_v[i], ln_g[i, 1], ln_b[i, 1], tt, tt_dense, eb)
    return x
```

```python
import functools
import math

import jax
import jax.numpy as jnp
from jax import lax
from jax.experimental import pallas as pl
from jax.experimental.pallas import tpu as pltpu

F32 = jnp.float32
BF16 = jnp.bfloat16
HI = lax.Precision.HIGHEST

HEAD = 64
PAIR = 2 * HEAD
SUB = 8
LANES = 128
CHUNK = 64
GN_EPS = 64e-5
LN_EPS = 1e-5
DEPTH = 2
ALPHA = (2 * DEPTH) ** 0.25
POOL_WINDOWS = (2, 4, 8, 16)
POOL_HALO = 16
N_KEYS = 128
PEER_HEADS = 8
PEER_TOPK = 16
INV_SQRT2 = 0.7071067811865476
NEG_INF = float("-inf")
V7X_VMEM_LIMIT = 56 * 1024 * 1024

NT_DIMS = (((1,), (1,)), ((), ()))
TN_DIMS = (((0,), (0,)), ((), ()))


def _dot(a, b, precision=None):
    return jnp.dot(a, b, precision=precision, preferred_element_type=F32)


def _dot_nt(a, b):
    return lax.dot_general(a, b, NT_DIMS, preferred_element_type=F32)


def _dot_tn(a, b):
    return lax.dot_general(a, b, TN_DIMS, preferred_element_type=F32)


def _split_bf16(x):
    hi = x.astype(BF16)
    return hi, (x - hi.astype(F32)).astype(BF16)


def _dot_sel(x, sel):
    hi, lo = _split_bf16(x)
    sel = sel.astype(BF16)
    return _dot(hi, sel) + _dot(lo, sel)


def _sel_dot(sel, x):
    hi, lo = _split_bf16(x)
    sel = sel.astype(BF16)
    return _dot(sel, hi) + _dot(sel, lo)


def _params(sem, vmem=V7X_VMEM_LIMIT, flags=None):
    return pltpu.CompilerParams(dimension_semantics=sem, vmem_limit_bytes=vmem, flags=flags)


def _layer_norm_rows(z, g, b):
    mu = jnp.mean(z, axis=-1, keepdims=True)
    d = z - mu
    var = jnp.mean(d * d, axis=-1, keepdims=True)
    return d * lax.rsqrt(var + LN_EPS) * g + b


def _mod_kernel(c_ref, w_ref, b_ref, o_ref):
    c = c_ref[...]
    cond = c * jax.nn.sigmoid(c)
    o_ref[0] = _dot(cond, w_ref[0], HI) + b_ref[0]


def _adaln_mod(c, ada_w, ada_b):
    depth, d, nd = ada_w.shape
    b = c.shape[0]
    nmod = nd // d
    out = pl.pallas_call(
        _mod_kernel,
        name="adaln_mod",
        grid=(depth, nmod),
        in_specs=[
            pl.BlockSpec((b, d), lambda l, n: (0, 0)),
            pl.BlockSpec((1, d, d), lambda l, n: (l, 0, n)),
            pl.BlockSpec((1, 1, d), lambda l, n: (l, 0, n)),
        ],
        out_specs=pl.BlockSpec((1, b, d), lambda l, n: (l, 0, n)),
        out_shape=jax.ShapeDtypeStruct((depth, b, nd), F32),
        compiler_params=_params(("arbitrary", "arbitrary")),
    )(c, ada_w, ada_b.reshape(depth, 1, nd))
    return out.reshape(depth, b, nmod, d)


def _softplus(z):
    return jnp.maximum(z, 0.0) + jnp.log1p(jnp.exp(-jnp.abs(z)))


def _rwkv_proj_kernel(x_ref, xp_ref, mod_ref, mu_ref, vec_ref, wrkv_ref, w1_ref, w2_ref, a1_ref, a2_ref,
                      g1_ref, g2_ref, r_ref, lw_ref, k_ref, v_ref, kk_ref, a_ref, g_ref):
    s = pl.program_id(1)
    m = mod_ref[0]
    shift, scale = m[0:1], m[1:2]
    h = x_ref[0] * (1.0 + scale) + shift
    prev = xp_ref[0][7:8] * (1.0 + scale) + shift
    prev = jnp.where(s == 0, 0.0, prev)
    row = lax.broadcasted_iota(jnp.int32, h.shape, 0)
    hprev = jnp.where(row == 0, prev, pltpu.roll(h, 1, 0))
    xx = hprev - h
    mu = mu_ref[...]

    def mix(n):
        return (h + xx * mu[n:n + 1]).astype(BF16)

    vec = vec_ref[...]
    w0, a0, k_k, k_a = vec[0:1], vec[1:2], vec[2:3], vec[3:4]
    r = _dot(mix(0), wrkv_ref[0])
    k = _dot(mix(1), wrkv_ref[1])
    v = _dot(mix(2), wrkv_ref[2])
    wl = w0 + _dot(jnp.tanh(_dot(mix(3), w1_ref[...])).astype(BF16), w2_ref[...])
    w = -_softplus(-wl) - 0.5
    a = jax.nn.sigmoid(a0 + _dot(_dot(mix(4), a1_ref[...]).astype(BF16), a2_ref[...]))
    g = _dot(jax.nn.sigmoid(_dot(mix(5), g1_ref[...])).astype(BF16), g2_ref[...])
    r_ref[0] = r.astype(BF16)
    lw_ref[0] = -jnp.exp(w)
    k_ref[0] = (k * (1.0 + (a - 1.0) * k_a)).astype(BF16)
    v_ref[0] = v.astype(BF16)
    kk_ref[0] = (k * k_k).astype(BF16)
    a_ref[0] = a.astype(BF16)
    g_ref[0] = g.astype(BF16)


def _pad_cols(w, n):
    return jnp.pad(w, ((0, 0), (0, n - w.shape[1])))


def _pad_rows(w, n):
    return jnp.pad(w, ((0, n - w.shape[0]), (0, 0)))


def _rwkv_proj(x, mod_l, mu, w_rkv, w0, w1, w2, a0, a1, a2, g1, g2, k_k, k_a, ts):
    b, s, d = x.shape
    lora = 128
    glora = 256
    vec = jnp.zeros((8, d), F32).at[0].set(w0).at[1].set(a0).at[2].set(k_k).at[3].set(k_a)
    tile = pl.BlockSpec((1, ts, d), lambda i, j: (i, j, 0))
    full2 = lambda shape: pl.BlockSpec(shape, lambda i, j: (0, 0))
    outs = pl.pallas_call(
        _rwkv_proj_kernel,
        name="rwkv_proj",
        grid=(b, s // ts),
        in_specs=[
            tile,
            pl.BlockSpec((1, 8, d), lambda i, j: (i, jnp.maximum(j * (ts // 8) - 1, 0), 0)),
            pl.BlockSpec((1, 6, d), lambda i, j: (i, 0, 0)),
            full2((6, d)),
            full2((8, d)),
            pl.BlockSpec((3, d, d), lambda i, j: (0, 0, 0)),
            full2((d, lora)), full2((lora, d)),
            full2((d, lora)), full2((lora, d)),
            full2((d, glora)), full2((glora, d)),
        ],
        out_specs=[tile] * 7,
        out_shape=[jax.ShapeDtypeStruct((b, s, d), F32 if n == 1 else BF16) for n in range(7)],
        compiler_params=_params(("arbitrary", "arbitrary")),
    )(x, x, mod_l, mu, vec, w_rkv.astype(BF16),
      _pad_cols(w1, lora).astype(BF16), _pad_rows(w2, lora).astype(BF16),
      _pad_cols(a1, lora).astype(BF16), _pad_rows(a2, lora).astype(BF16),
      _pad_cols(g1, glora).astype(BF16), _pad_rows(g2, glora).astype(BF16))
    return outs


def _head_masks(shape):
    lane = lax.broadcasted_iota(jnp.int32, shape, len(shape) - 1)
    first = (lane % PAIR) < HEAD
    return first, jnp.logical_not(first)


def _stack_heads(z):
    m0, m1 = _head_masks(z.shape)
    return jnp.concatenate([jnp.where(m0, z, 0.0), jnp.where(m1, z, 0.0)], axis=0)


def _wkv_kernel(r_ref, lw_ref, k_ref, v_ref, kk_ref, a_ref, rk_ref, lng_ref, lnb_ref, y_ref,
                s_ref, yb_ref, *, nchunk):
    L = CHUNK

    @pl.when(pl.program_id(2) == 0)
    def _():
        s_ref[...] = jnp.zeros_like(s_ref)

    r2 = lax.broadcasted_iota(jnp.int32, (PAIR, PAIR), 0)
    c2 = lax.broadcasted_iota(jnp.int32, (PAIR, PAIR), 1)
    same_head = (r2 // HEAD) == (c2 // HEAD)
    eye = r2 == c2
    bd_ones = jnp.where(same_head, 1.0, 0.0).astype(F32)
    tr = lax.broadcasted_iota(jnp.int32, (L, PAIR), 0)
    tc = lax.broadcasted_iota(jnp.int32, (L, PAIR), 1) % HEAD
    strict = tr > tc
    incl = tr >= tc

    def off_diag(size):
        same = (tr // (2 * size)) == (tc // (2 * size))
        return same & ((tr % (2 * size)) >= size) & ((tc % (2 * size)) < size)
    lr = lax.broadcasted_iota(jnp.int32, (L, L), 0)
    lc = lax.broadcasted_iota(jnp.int32, (L, L), 1)
    ltri = jnp.where(lr >= lc, 1.0, 0.0).astype(F32)

    chunks = range(nchunk)
    r_all = r_ref[0].astype(F32)
    lw_all = lw_ref[0]
    k_all = k_ref[0].astype(F32)
    v_all = v_ref[0].astype(F32)
    a_all = a_ref[0].astype(F32)
    kkr = kk_ref[0].astype(F32)
    kk = kkr * lax.rsqrt(jnp.maximum(_dot_sel(kkr * kkr, bd_ones), 1e-24))
    bv_all = kk * a_all

    def rows(x, c):
        return x[c * L:(c + 1) * L]

    cs = [_sel_dot(ltri, rows(lw_all, c)) for c in chunks]
    rt, at, vst, lhs, rhs, bc, kc, p_last = [], [], [], [], [], [], [], []
    for c in chunks:
        cs_last = cs[c][L - 1:L, :]
        pinv = jnp.exp(-cs[c])
        prem = jnp.exp(cs_last - cs[c])
        rt.append(rows(r_all, c) * jnp.exp(cs[c]))
        at.append(-rows(kk, c) * jnp.exp(cs[c] - rows(lw_all, c)))
        bt = rows(bv_all, c) * pinv
        kt = rows(k_all, c) * pinv
        bc.append((rows(bv_all, c) * prem).astype(BF16))
        kc.append((rows(k_all, c) * prem).astype(BF16))
        p_last.append(jnp.exp(cs_last))
        vst.append(_stack_heads(rows(v_all, c)).astype(BF16))
        lhs.append(jnp.concatenate([at[c], rt[c]], axis=0).astype(BF16))
        rhs.append(jnp.concatenate([_stack_heads(bt), _stack_heads(kt)], axis=0).astype(BF16))
    o = [_dot_nt(lhs[c], rhs[c]) for c in chunks]
    nmat = [jnp.where(strict, o[c][:L, :PAIR], 0.0) for c in chunks]
    akv = [_dot(jnp.where(strict, o[c][:L, PAIR:], 0.0).astype(BF16), vst[c]) for c in chunks]
    ident = jnp.where(tr == tc, 1.0, 0.0)
    tinv = [ident + jnp.where(off_diag(1), nmat[c], 0.0) for c in chunks]
    size = 2
    while size < L:
        mask = off_diag(size)
        tn = [_dot(tinv[c].astype(BF16), _stack_heads(jnp.where(mask, nmat[c], 0.0)).astype(BF16)) for c in chunks]
        tinv = [tinv[c] + _dot(tn[c].astype(BF16), _stack_heads(tinv[c]).astype(BF16)) for c in chunks]
        size *= 2
    z = [_dot(tinv[c].astype(BF16), _stack_heads(jnp.concatenate([at[c], akv[c]], axis=1)).astype(BF16))
         for c in chunks]
    tmp = [_dot(jnp.where(incl, o[c][L:, :PAIR], 0.0).astype(BF16), _stack_heads(z[c]).astype(BF16))
           for c in chunks]
    rkv = [_dot(jnp.where(incl, o[c][L:, PAIR:], 0.0).astype(BF16), vst[c]) for c in chunks]
    bz = [_dot_tn(bc[c], z[c].astype(BF16)) for c in chunks]
    kv = [_dot_tn(kc[c], rows(v_all, c).astype(BF16)) for c in chunks]
    state = s_ref[...]
    for c in chunks:
        sb = state.astype(BF16)
        rh = (rt[c] + tmp[c][:, :PAIR]).astype(BF16)
        yb_ref[c * L:(c + 1) * L, :] = _dot(rh, sb) + tmp[c][:, PAIR:] + rkv[c]
        m_c = jnp.where(eye, jnp.broadcast_to(p_last[c], (PAIR, PAIR)), 0.0) + jnp.where(same_head, bz[c][:, :PAIR], 0.0)
        state = _dot(m_c.astype(BF16), sb) + jnp.where(same_head, bz[c][:, PAIR:] + kv[c], 0.0)
    s_ref[...] = state

    y = yb_ref[...]
    bd_avg = bd_ones * (1.0 / HEAD)
    ym = _dot_sel(y, bd_avg)
    d = y - ym
    yv = _dot_sel(d * d, bd_avg)
    yn = d * lax.rsqrt(yv + GN_EPS) * lng_ref[...] + lnb_ref[...]
    bonus = _dot_sel(r_all * k_all * rk_ref[...], bd_ones) * v_all
    y_ref[0] = yn + bonus


def _wkv_scan(r, lw, k, v, kkr, a, r_k, lnx_g, lnx_b, tc):
    b, s, d = r.shape
    nchunk = tc // CHUNK
    tile = pl.BlockSpec((1, tc, PAIR), lambda i, p, j: (i, j, p))
    row = pl.BlockSpec((1, PAIR), lambda i, p, j: (0, p))
    return pl.pallas_call(
        functools.partial(_wkv_kernel, nchunk=nchunk),
        name="wkv_scan",
        grid=(b, d // PAIR, s // tc),
        in_specs=[tile] * 6 + [row] * 3,
        out_specs=tile,
        out_shape=jax.ShapeDtypeStruct((b, s, d), F32),
        scratch_shapes=[
            pltpu.VMEM((PAIR, PAIR), F32),
            pltpu.VMEM((tc, PAIR), F32),
        ],
        compiler_params=_params(("arbitrary", "arbitrary", "arbitrary")),
    )(r, lw, k, v, kkr, a, r_k.reshape(1, d), lnx_g.reshape(1, d), lnx_b.reshape(1, d))


def _mix_out_kernel(y_ref, g_ref, x_ref, mod_ref, wo_ref, ln_ref, o_ref):
    m = mod_ref[0]
    gate = m[2:3]
    yg = (y_ref[0] * g_ref[0].astype(F32)).astype(BF16)
    o = _dot(yg, wo_ref[...])
    z = ALPHA * x_ref[0] + gate * o
    ln = ln_ref[...]
    o_ref[0] = _layer_norm_rows(z, ln[0:1], ln[1:2])


def _mix_out(y, g, x, mod_l, w_o, ln_g, ln_b, ts):
    b, s, d = x.shape
    tile = pl.BlockSpec((1, ts, d), lambda i, j: (i, j, 0))
    ln = jnp.zeros((8, d), F32).at[0].set(ln_g).at[1].set(ln_b)
    return pl.pallas_call(
        _mix_out_kernel,
        name="mix_out",
        grid=(b, s // ts),
        in_specs=[tile, tile, tile,
                  pl.BlockSpec((1, 6, d), lambda i, j: (i, 0, 0)),
                  pl.BlockSpec((d, d), lambda i, j: (0, 0)),
                  pl.BlockSpec((8, d), lambda i, j: (0, 0))],
        out_specs=tile,
        out_shape=jax.ShapeDtypeStruct((b, s, d), F32),
        compiler_params=_params(("arbitrary", "arbitrary")),
    )(y, g, x, mod_l, w_o.astype(BF16), ln)


def _pool_kernel(x_ref, mod_ref, win_ref, wgrp_ref, sc_ref, wout_ref, ln_ref, o_ref, zext_ref, *, ts):
    s = pl.program_id(1)
    m = mod_ref[0]
    shift, scale, gate = m[0:1], m[1:2], m[2:3]
    x = x_ref[0]
    h = x * (1.0 + scale) + shift
    z = _dot(h.astype(BF16), win_ref[...])

    @pl.when(s == 0)
    def _():
        zext_ref[0:POOL_HALO, :] = jnp.zeros((POOL_HALO, z.shape[1]), F32)

    zext_ref[POOL_HALO:POOL_HALO + ts, :] = z
    pos = s * ts + lax.broadcasted_iota(jnp.int32, (ts, 1), 0)
    gw = z.shape[1] // len(POOL_WINDOWS)
    parts = []
    for gi, win in enumerate(POOL_WINDOWS):
        lo = gi * gw
        zg = z[:, lo:lo + gw]
        acc = zg
        for back in range(1, win):
            acc = acc + zext_ref[POOL_HALO - back:POOL_HALO - back + ts, lo:lo + gw]
        cnt = jnp.minimum(pos + 1, win).astype(F32)
        p = acc / cnt - zg
        parts.append(_dot(p.astype(BF16), wgrp_ref[gi]))
    y = jnp.concatenate(parts, axis=1) * sc_ref[...]
    o = _dot(y.astype(BF16), wout_ref[...])
    zext_ref[0:POOL_HALO, :] = zext_ref[ts:ts + POOL_HALO, :]
    ln = ln_ref[...]
    o_ref[0] = _layer_norm_rows(ALPHA * x + gate * o, ln[0:1], ln[1:2])


def _pool_mix(x, mod_l, w_in, w_grp, scale, w_out, ln_g, ln_b, ts):
    b, s, d = x.shape
    ng, gw, _ = w_grp.shape
    tile = pl.BlockSpec((1, ts, d), lambda i, j: (i, j, 0))
    ln = jnp.zeros((8, d), F32).at[0].set(ln_g).at[1].set(ln_b)
    return pl.pallas_call(
        functools.partial(_pool_kernel, ts=ts),
        name="pool_mix",
        grid=(b, s // ts),
        in_specs=[tile,
                  pl.BlockSpec((1, 6, d), lambda i, j: (i, 0, 0)),
                  pl.BlockSpec((d, d), lambda i, j: (0, 0)),
                  pl.BlockSpec((ng, gw, gw), lambda i, j: (0, 0, 0)),
                  pl.BlockSpec((1, d), lambda i, j: (0, 0)),
                  pl.BlockSpec((d, d), lambda i, j: (0, 0)),
                  pl.BlockSpec((8, d), lambda i, j: (0, 0))],
        out_specs=tile,
        out_shape=jax.ShapeDtypeStruct((b, s, d), F32),
        scratch_shapes=[pltpu.VMEM((ts + POOL_HALO, d), F32)],
        compiler_params=_params(("arbitrary", "arbitrary")),
    )(x, mod_l, w_in.astype(BF16), w_grp.astype(BF16), scale.reshape(1, d), w_out.astype(BF16), ln)


def _fold_kernel(keys_ref, wq_ref, o_ref):
    o_ref[0] = lax.dot_general(keys_ref[0, 0], wq_ref[0], NT_DIMS, precision=HI,
                               preferred_element_type=F32).astype(BF16)


def _peer_fold(keys, w_q):
    depth, nh, two, nk, dk = keys.shape
    d = w_q.shape[1]
    nb = nh * two
    return pl.pallas_call(
        _fold_kernel,
        name="peer_fold",
        grid=(depth, nb),
        in_specs=[pl.BlockSpec((1, 1, nk, dk), lambda l, j: (l, j, 0, 0)),
                  pl.BlockSpec((1, d, dk), lambda l, j: (l, 0, j))],
        out_specs=pl.BlockSpec((1, nk, d), lambda l, j: (l, j, 0)),
        out_shape=jax.ShapeDtypeStruct((depth, nb * nk, d), BF16),
        compiler_params=_params(("arbitrary", "arbitrary")),
    )(keys.reshape(depth, nb, nk, dk), w_q)


N_TOP = PEER_TOPK + 1
CAND = [(a, b) for a in range(N_TOP) for b in range(N_TOP) if (a + 1) * (b + 1) <= N_TOP]
CAND_ROWS = 1 << (len(CAND) - 1).bit_length()


def _sorting_network(n):
    pairs = []

    def merge(lo, cnt, step):
        nxt = step * 2
        if nxt < cnt:
            merge(lo, cnt, nxt)
            merge(lo + step, cnt, nxt)
            for i in range(lo + step, lo + cnt - step, nxt):
                pairs.append((i, i + step))
        else:
            pairs.append((lo, lo + step))

    def sort(lo, cnt):
        if cnt > 1:
            half = cnt // 2
            sort(lo, half)
            sort(lo + half, half)
            merge(lo, cnt, 1)

    sort(0, n)
    return pairs


def _top_sorted(s, n):
    if s.shape[1] > LANES:
        parts = [_top_sorted(s[:, c:c + LANES], n) for c in range(0, s.shape[1], LANES)]
        return [jnp.concatenate([p[k] for p in parts], axis=1) for k in range(n)]
    groups = s.shape[0] // SUB
    cols = [s[g * SUB:(g + 1) * SUB, :] for g in range(groups)]
    for i, j in _sorting_network(groups):
        cols[i], cols[j] = jnp.maximum(cols[i], cols[j]), jnp.minimum(cols[i], cols[j])
    sub = lax.broadcasted_iota(jnp.int32, cols[0].shape, 0)
    outs = []
    for it in range(n):
        mx = jnp.max(cols[0], axis=0, keepdims=True)
        outs.append(mx)
        left = n - 1 - it
        if left == 0:
            break
        first = jnp.min(jnp.where(cols[0] == mx, sub, SUB), axis=0, keepdims=True)
        pop = sub == first
        for lvl in range(min(groups, left)):
            below = cols[lvl + 1] if lvl + 1 < groups else NEG_INF
            cols[lvl] = jnp.where(pop, below, cols[lvl])
    return outs


def _twin_bf16_words(x):
    bits = pltpu.bitcast(x.astype(BF16).astype(F32), jnp.uint32)
    return bits | (bits >> 16)


def _route_kernel(x_ref, mod_ref, wf_ref, h2_ref, e2_ref, rk_ref, c1_ref, n1_ref):
    hd = pl.program_id(1)
    m = mod_ref[0]
    shift, scale = m[3:4], m[4:5]

    @pl.when(hd == 0)
    def _():
        h2_ref[...] = (x_ref[...] * (1.0 + scale) + shift).T.astype(BF16)

    h2 = h2_ref[...]
    s1 = _dot(wf_ref[0:N_KEYS, :], h2)
    s2 = _dot(wf_ref[N_KEYS:2 * N_KEYS, :], h2)
    ta = _top_sorted(s1, N_TOP)
    tb = _top_sorted(s2, N_TOP)
    rows = [ta[a] + tb[b] for a, b in CAND]
    rows += [jnp.full_like(rows[0], NEG_INF)] * (CAND_ROWS - len(CAND))
    cand = jnp.concatenate(rows, axis=0)
    tc = _top_sorted(cand, N_TOP)
    tau = 0.5 * (tc[PEER_TOPK - 1] + tc[PEER_TOPK])
    top = ta[0] + tb[0]
    zsum = jnp.sum(jnp.where(cand >= tau, jnp.exp(cand - top), 0.0), axis=0, keepdims=True)
    need = tau - s1
    rank2 = jnp.zeros_like(s2)
    count1 = jnp.zeros_like(s1)
    for b in range(PEER_TOPK):
        rank2 = jnp.where(tb[b] > s2, b + 1.0, rank2)
        count1 = jnp.where(tb[b] >= need, b + 1.0, count1)
    e2_ref[0] = jnp.exp(s2 - tb[0]).astype(BF16)
    rk_ref[0] = rank2.astype(BF16)
    c1w = _twin_bf16_words(jnp.exp(s1 - ta[0]) / zsum)
    n1w = _twin_bf16_words(count1)
    for blk in range(c1_ref.shape[1]):
        c1_ref[0, blk] = c1w[:, blk * LANES:(blk + 1) * LANES]
        n1_ref[0, blk] = n1w[:, blk * LANES:(blk + 1) * LANES]


def _peer_route(x2d, mod_l, wf, tt, tiles_per_batch):
    t, d = x2d.shape
    nh = PEER_HEADS
    per_head = pl.BlockSpec((1, N_KEYS, tt), lambda i, h: (h, 0, i))
    per_key = pl.BlockSpec((1, tt // LANES, N_KEYS, LANES), lambda i, h: (h, i, 0, 0))
    return pl.pallas_call(
        _route_kernel,
        name="peer_route",
        grid=(t // tt, nh),
        in_specs=[pl.BlockSpec((tt, d), lambda i, h: (i, 0)),
                  pl.BlockSpec((1, 6, d), lambda i, h: (i // tiles_per_batch, 0, 0)),
                  pl.BlockSpec((2 * N_KEYS, d), lambda i, h: (h, 0))],
        out_specs=[pl.BlockSpec((d, tt), lambda i, h: (0, i)), per_head, per_head, per_key, per_key],
        out_shape=[jax.ShapeDtypeStruct((d, t), BF16),
                   jax.ShapeDtypeStruct((nh, N_KEYS, t), BF16),
                   jax.ShapeDtypeStruct((nh, N_KEYS, t), BF16),
                   jax.ShapeDtypeStruct((nh, t // LANES, N_KEYS, LANES), jnp.uint32),
                   jax.ShapeDtypeStruct((nh, t // LANES, N_KEYS, LANES), jnp.uint32)],
        compiler_params=_params(("arbitrary", "arbitrary")),
    )(x2d, mod_l, wf)


ROWS = 16
TOKEN_COLS = 256
MXU_ROWS = 256
MXU_COLS = 256


def _dense_kernel(h2_ref, e2_ref, rk_ref, c1_ref, n1_ref, u_ref, vt_ref, x_ref, mod_ref, ln_ref, o_ref,
                  acc_ref, za_ref, zb_ref, pa_ref, pb_ref, *, eb):
    e = pl.program_id(1)
    last = pl.num_programs(1) - 1
    tt = h2_ref.shape[1]
    slabs = eb // N_KEYS

    @pl.when(e == 0)
    def _():
        acc_ref[...] = jnp.zeros_like(acc_ref)
        zb_ref[...] = jnp.zeros_like(zb_ref)
        pa_ref[...] = jnp.zeros_like(pa_ref)

    def gate_blocks(z_ref, p_ref):
        blocks = []
        for il in range(slabs):
            i1 = jnp.clip((e - 1) * slabs + il, 0, N_KEYS - 1)
            for t0 in range(0, tt, TOKEN_COLS):
                cols = slice(t0, t0 + TOKEN_COLS)
                rows_of = {}

                def token_rows(i1=i1, t0=t0, rows_of=rows_of):
                    if not rows_of:
                        def tile_of(ref, hd):
                            words = [jnp.broadcast_to(ref[hd, tb, pl.ds(i1, 1), :], (ROWS // 2, LANES))
                                     for tb in range(t0 // LANES, (t0 + TOKEN_COLS) // LANES)]
                            return pltpu.bitcast(jnp.concatenate(words, axis=1), BF16)
                        rows_of["cnt"] = [tile_of(n1_ref, hd) for hd in range(PEER_HEADS)]
                        rows_of["wgt"] = [tile_of(c1_ref, hd) for hd in range(PEER_HEADS)]
                    return rows_of["cnt"], rows_of["wgt"]

                for jb in range(N_KEYS // ROWS):
                    def block(il=il, jb=jb, cols=cols, token_rows=token_rows):
                        cnt, wgt = token_rows()
                        r0 = il * N_KEYS + jb * ROWS
                        zz = z_ref[r0:r0 + ROWS, cols].astype(BF16)
                        act = 0.5 * zz * (1.0 + lax.erf(zz * INV_SQRT2))
                        gsum = jnp.zeros((ROWS, TOKEN_COLS), BF16)
                        for hd in range(PEER_HEADS):
                            keep = rk_ref[hd, jb * ROWS:(jb + 1) * ROWS, cols] < cnt[hd]
                            gsum = gsum + jnp.where(keep, e2_ref[hd, jb * ROWS:(jb + 1) * ROWS, cols], 0.0) * wgt[hd]
                        p_ref[r0:r0 + ROWS, cols] = act * gsum
                    blocks.append(block)
        return blocks

    def matmul_units(z_new, p_old):
        units = []
        for n0 in range(0, tt, MXU_COLS):
            for m0 in range(0, acc_ref.shape[0], MXU_ROWS):
                def unit(m0=m0, n0=n0):
                    acc_ref[m0:m0 + MXU_ROWS, n0:n0 + MXU_COLS] += _dot(vt_ref[0, m0:m0 + MXU_ROWS, :], p_old[:, n0:n0 + MXU_COLS])
                units.append((eb, unit))
            for m0 in range(0, eb, MXU_ROWS):
                def unit(m0=m0, n0=n0):
                    z_new[m0:m0 + MXU_ROWS, n0:n0 + MXU_COLS] = _dot(u_ref[m0:m0 + MXU_ROWS, :], h2_ref[:, n0:n0 + MXU_COLS])
                units.append((u_ref.shape[1], unit))
        return units

    def step(z_new, z_old, p_new, p_old):
        blocks = gate_blocks(z_old, p_new)
        units = matmul_units(z_new, p_old)
        total = sum(k for k, _ in units)
        done = 0
        issued = 0
        for k, unit in units:
            unit()
            done += k
            upto = -(-len(blocks) * done // total)
            for block in blocks[issued:upto]:
                block()
            issued = upto

    @pl.when(e % 2 == 0)
    def _():
        step(za_ref, zb_ref, pb_ref, pa_ref)

    @pl.when(e % 2 == 1)
    def _():
        step(zb_ref, za_ref, pa_ref, pb_ref)

    @pl.when(e == last)
    def _():
        m = mod_ref[0]
        gate = m[5:6]
        ln = ln_ref[...]
        y = acc_ref[...].T
        o_ref[...] = _layer_norm_rows(ALPHA * x_ref[...] + gate * y, ln[0:1], ln[1:2])


def _peer_dense(h2, e2, rk2, c1, n1, u_bf, vt_bf, x2d, mod_l, ln_g, ln_b, tt, eb, tiles_per_batch):
    t, d = x2d.shape
    nh = PEER_HEADS
    ln = jnp.zeros((8, d), F32).at[0].set(ln_g).at[1].set(ln_b)
    nslab = u_bf.shape[0] // eb
    per_tile = pl.BlockSpec((nh, N_KEYS, tt), lambda i, e: (0, 0, i))
    per_key = pl.BlockSpec((nh, tt // LANES, N_KEYS, LANES), lambda i, e: (0, i, 0, 0))
    return pl.pallas_call(
        functools.partial(_dense_kernel, eb=eb),
        name="peer_dense",
        grid=(t // tt, nslab + 2),
        in_specs=[pl.BlockSpec((d, tt), lambda i, e: (0, i)),
                  per_tile, per_tile, per_key, per_key,
                  pl.BlockSpec((eb, d), lambda i, e: (jnp.minimum(e, nslab - 1), 0)),
                  pl.BlockSpec((1, d, eb), lambda i, e: (jnp.clip(e - 2, 0, nslab - 1), 0, 0)),
                  pl.BlockSpec((tt, d), lambda i, e: (i, 0)),
                  pl.BlockSpec((1, 6, d), lambda i, e: (i // tiles_per_batch, 0, 0)),
                  pl.BlockSpec((8, d), lambda i, e: (0, 0))],
        out_specs=pl.BlockSpec((tt, d), lambda i, e: (i, 0)),
        out_shape=jax.ShapeDtypeStruct((t, d), F32),
        scratch_shapes=[pltpu.VMEM((d, tt), F32),
                        pltpu.VMEM((eb, tt), F32),
                        pltpu.VMEM((eb, tt), F32),
                        pltpu.VMEM((eb, tt), BF16),
                        pltpu.VMEM((eb, tt), BF16)],
        compiler_params=_params(("arbitrary", "arbitrary")),
    )(h2, e2, rk2, c1, n1, u_bf, vt_bf, x2d, mod_l, ln)


def _peer_ffn(x, mod_l, wf, u_tab, v_tab, ln_g, ln_b, tt_route, tt_dense, eb):
    b, s, d = x.shape
    x2d = x.reshape(b * s, d)
    h2, e2, rk2, c1, n1 = _peer_route(x2d, mod_l, wf, tt_route, s // tt_route)
    vt = v_tab.astype(BF16).reshape(v_tab.shape[0] // eb, eb, d).transpose(0, 2, 1)
    out = _peer_dense(h2, e2, rk2, c1, n1, u_tab.astype(BF16), vt, x2d, mod_l, ln_g, ln_b,
                      tt_dense, eb, s // tt_dense)
    return out.reshape(b, s, d)


def kernel(x, c, ada_w, ada_b, ln_g, ln_b, rw_mu, rw_w_rkv, rw_w0, rw_w1, rw_w2, rw_a0, rw_a1, rw_a2, rw_g1, rw_g2, rw_k_k, rw_k_a, rw_r_k, rw_lnx_g, rw_lnx_b, rw_w_o, pl_w_in, pl_w_grp, pl_scale, pl_w_out, pe_w_q, pe_keys, pe_u, pe_v):
    b, s, d = x.shape
    depth = ada_w.shape[0]
    ts = min(256, s)
    tc = min(1024, s)
    tt = min(512, s)
    tt_dense = min(512, s)
    eb = 512
    mod = _adaln_mod(c, ada_w, ada_b)
    wf = _peer_fold(pe_keys, pe_w_q)
    for i in range(depth):
        j = i // 2
        if i % 2 == 0:
            r, lw, k, v, kkr, a, g = _rwkv_proj(x, mod[i], rw_mu[j], rw_w_rkv[j], rw_w0[j], rw_w1[j], rw_w2[j],
                                                rw_a0[j], rw_a1[j], rw_a2[j], rw_g1[j], rw_g2[j],
                                                rw_k_k[j], rw_k_a[j], ts)
            y = _wkv_scan(r, lw, k, v, kkr, a, rw_r_k[j], rw_lnx_g[j], rw_lnx_b[j], tc)
            x = _mix_out(y, g, x, mod[i], rw_w_o[j], ln_g[i, 0], ln_b[i, 0], ts)
        else:
            x = _pool_mix(x, mod[i], pl_w_in[j], pl_w_grp[j], pl_scale[j], pl_w_out[j],
                          ln_g[i, 0], ln_b[i, 0], ts)
        x = _peer_ffn(x, mod[i], wf[i], pe_u[i], pe_v[i], ln_g[i, 1], ln_b[i, 1], tt, tt_dense, eb)
    return x
```

```python
import functools
import math

import jax
import jax.numpy as jnp
from jax import lax
from jax.experimental import pallas as pl
from jax.experimental.pallas import tpu as pltpu

F32 = jnp.float32
BF16 = jnp.bfloat16
HI = lax.Precision.HIGHEST

HEAD = 64
PAIR = 2 * HEAD
SUB = 8
LANES = 128
CHUNK = 64
GN_EPS = 64e-5
LN_EPS = 1e-5
DEPTH = 2
ALPHA = (2 * DEPTH) ** 0.25
POOL_WINDOWS = (2, 4, 8, 16)
POOL_HALO = 16
N_KEYS = 128
PEER_HEADS = 8
PEER_TOPK = 16
INV_SQRT2 = 0.7071067811865476
NEG_INF = float("-inf")
V7X_VMEM_LIMIT = 56 * 1024 * 1024

NT_DIMS = (((1,), (1,)), ((), ()))
TN_DIMS = (((0,), (0,)), ((), ()))


def _dot(a, b, precision=None):
    return jnp.dot(a, b, precision=precision, preferred_element_type=F32)


def _dot_nt(a, b):
    return lax.dot_general(a, b, NT_DIMS, preferred_element_type=F32)


def _dot_tn(a, b):
    return lax.dot_general(a, b, TN_DIMS, preferred_element_type=F32)


def _split_bf16(x):
    hi = x.astype(BF16)
    return hi, (x - hi.astype(F32)).astype(BF16)


def _dot_sel(x, sel):
    hi, lo = _split_bf16(x)
    sel = sel.astype(BF16)
    return _dot(hi, sel) + _dot(lo, sel)


def _sel_dot(sel, x):
    hi, lo = _split_bf16(x)
    sel = sel.astype(BF16)
    return _dot(sel, hi) + _dot(sel, lo)


def _params(sem, vmem=V7X_VMEM_LIMIT, flags=None):
    return pltpu.CompilerParams(dimension_semantics=sem, vmem_limit_bytes=vmem, flags=flags)


def _layer_norm_rows(z, g, b):
    mu = jnp.mean(z, axis=-1, keepdims=True)
    d = z - mu
    var = jnp.mean(d * d, axis=-1, keepdims=True)
    return d * lax.rsqrt(var + LN_EPS) * g + b


def _mod_kernel(c_ref, w_ref, b_ref, o_ref):
    c = c_ref[...]
    cond = c * jax.nn.sigmoid(c)
    o_ref[0] = _dot(cond, w_ref[0], HI) + b_ref[0]


def _adaln_mod(c, ada_w, ada_b):
    depth, d, nd = ada_w.shape
    b = c.shape[0]
    nmod = nd // d
    out = pl.pallas_call(
        _mod_kernel,
        name="adaln_mod",
        grid=(depth, nmod),
        in_specs=[
            pl.BlockSpec((b, d), lambda l, n: (0, 0)),
            pl.BlockSpec((1, d, d), lambda l, n: (l, 0, n)),
            pl.BlockSpec((1, 1, d), lambda l, n: (l, 0, n)),
        ],
        out_specs=pl.BlockSpec((1, b, d), lambda l, n: (l, 0, n)),
        out_shape=jax.ShapeDtypeStruct((depth, b, nd), F32),
        compiler_params=_params(("arbitrary", "arbitrary")),
    )(c, ada_w, ada_b.reshape(depth, 1, nd))
    return out.reshape(depth, b, nmod, d)


def _softplus(z):
    return jnp.maximum(z, 0.0) + jnp.log1p(jnp.exp(-jnp.abs(z)))


def _rwkv_proj_kernel(x_ref, xp_ref, mod_ref, mu_ref, vec_ref, wrkv_ref, w1_ref, w2_ref, a1_ref, a2_ref,
                      g1_ref, g2_ref, r_ref, lw_ref, k_ref, v_ref, kk_ref, a_ref, g_ref):
    s = pl.program_id(1)
    m = mod_ref[0]
    shift, scale = m[0:1], m[1:2]
    h = x_ref[0] * (1.0 + scale) + shift
    prev = xp_ref[0][7:8] * (1.0 + scale) + shift
    prev = jnp.where(s == 0, 0.0, prev)
    row = lax.broadcasted_iota(jnp.int32, h.shape, 0)
    hprev = jnp.where(row == 0, prev, pltpu.roll(h, 1, 0))
    xx = hprev - h
    mu = mu_ref[...]

    def mix(n):
        return (h + xx * mu[n:n + 1]).astype(BF16)

    vec = vec_ref[...]
    w0, a0, k_k, k_a = vec[0:1], vec[1:2], vec[2:3], vec[3:4]
    r = _dot(mix(0), wrkv_ref[0])
    k = _dot(mix(1), wrkv_ref[1])
    v = _dot(mix(2), wrkv_ref[2])
    wl = w0 + _dot(jnp.tanh(_dot(mix(3), w1_ref[...])).astype(BF16), w2_ref[...])
    w = -_softplus(-wl) - 0.5
    a = jax.nn.sigmoid(a0 + _dot(_dot(mix(4), a1_ref[...]).astype(BF16), a2_ref[...]))
    g = _dot(jax.nn.sigmoid(_dot(mix(5), g1_ref[...])).astype(BF16), g2_ref[...])
    r_ref[0] = r.astype(BF16)
    lw_ref[0] = -jnp.exp(w)
    k_ref[0] = (k * (1.0 + (a - 1.0) * k_a)).astype(BF16)
    v_ref[0] = v.astype(BF16)
    kk_ref[0] = (k * k_k).astype(BF16)
    a_ref[0] = a.astype(BF16)
    g_ref[0] = g.astype(BF16)


def _pad_cols(w, n):
    return jnp.pad(w, ((0, 0), (0, n - w.shape[1])))


def _pad_rows(w, n):
    return jnp.pad(w, ((0, n - w.shape[0]), (0, 0)))


def _rwkv_proj(x, mod_l, mu, w_rkv, w0, w1, w2, a0, a1, a2, g1, g2, k_k, k_a, ts):
    b, s, d = x.shape
    lora = 128
    glora = 256
    vec = jnp.zeros((8, d), F32).at[0].set(w0).at[1].set(a0).at[2].set(k_k).at[3].set(k_a)
    tile = pl.BlockSpec((1, ts, d), lambda i, j: (i, j, 0))
    full2 = lambda shape: pl.BlockSpec(shape, lambda i, j: (0, 0))
    outs = pl.pallas_call(
        _rwkv_proj_kernel,
        name="rwkv_proj",
        grid=(b, s // ts),
        in_specs=[
            tile,
            pl.BlockSpec((1, 8, d), lambda i, j: (i, jnp.maximum(j * (ts // 8) - 1, 0), 0)),
            pl.BlockSpec((1, 6, d), lambda i, j: (i, 0, 0)),
            full2((6, d)),
            full2((8, d)),
            pl.BlockSpec((3, d, d), lambda i, j: (0, 0, 0)),
            full2((d, lora)), full2((lora, d)),
            full2((d, lora)), full2((lora, d)),
            full2((d, glora)), full2((glora, d)),
        ],
        out_specs=[tile] * 7,
        out_shape=[jax.ShapeDtypeStruct((b, s, d), F32 if n == 1 else BF16) for n in range(7)],
        compiler_params=_params(("arbitrary", "arbitrary")),
    )(x, x, mod_l, mu, vec, w_rkv.astype(BF16),
      _pad_cols(w1, lora).astype(BF16), _pad_rows(w2, lora).astype(BF16),
      _pad_cols(a1, lora).astype(BF16), _pad_rows(a2, lora).astype(BF16),
      _pad_cols(g1, glora).astype(BF16), _pad_rows(g2, glora).astype(BF16))
    return outs


def _head_masks(shape):
    lane = lax.broadcasted_iota(jnp.int32, shape, len(shape) - 1)
    first = (lane % PAIR) < HEAD
    return first, jnp.logical_not(first)


def _stack_heads(z):
    m0, m1 = _head_masks(z.shape)
    return jnp.concatenate([jnp.where(m0, z, 0.0), jnp.where(m1, z, 0.0)], axis=0)


def _wkv_kernel(r_ref, lw_ref, k_ref, v_ref, kk_ref, a_ref, rk_ref, lng_ref, lnb_ref, y_ref,
                s_ref, yb_ref, *, nchunk):
    L = CHUNK

    @pl.when(pl.program_id(2) == 0)
    def _():
        s_ref[...] = jnp.zeros_like(s_ref)

    r2 = lax.broadcasted_iota(jnp.int32, (PAIR, PAIR), 0)
    c2 = lax.broadcasted_iota(jnp.int32, (PAIR, PAIR), 1)
    same_head = (r2 // HEAD) == (c2 // HEAD)
    eye = r2 == c2
    bd_ones = jnp.where(same_head, 1.0, 0.0).astype(F32)
    tr = lax.broadcasted_iota(jnp.int32, (L, PAIR), 0)
    tc = lax.broadcasted_iota(jnp.int32, (L, PAIR), 1) % HEAD
    strict = tr > tc
    incl = tr >= tc

    def off_diag(size):
        same = (tr // (2 * size)) == (tc // (2 * size))
        return same & ((tr % (2 * size)) >= size) & ((tc % (2 * size)) < size)
    lr = lax.broadcasted_iota(jnp.int32, (L, L), 0)
    lc = lax.broadcasted_iota(jnp.int32, (L, L), 1)
    ltri = jnp.where(lr >= lc, 1.0, 0.0).astype(F32)

    chunks = range(nchunk)
    r_all = r_ref[0].astype(F32)
    lw_all = lw_ref[0]
    k_all = k_ref[0].astype(F32)
    v_all = v_ref[0].astype(F32)
    a_all = a_ref[0].astype(F32)
    kkr = kk_ref[0].astype(F32)
    kk = kkr * lax.rsqrt(jnp.maximum(_dot_sel(kkr * kkr, bd_ones), 1e-24))
    bv_all = kk * a_all

    def rows(x, c):
        return x[c * L:(c + 1) * L]

    cs = [_sel_dot(ltri, rows(lw_all, c)) for c in chunks]
    rt, at, vst, lhs, rhs, bc, kc, p_last = [], [], [], [], [], [], [], []
    for c in chunks:
        cs_last = cs[c][L - 1:L, :]
        pinv = jnp.exp(-cs[c])
        prem = jnp.exp(cs_last - cs[c])
        rt.append(rows(r_all, c) * jnp.exp(cs[c]))
        at.append(-rows(kk, c) * jnp.exp(cs[c] - rows(lw_all, c)))
        bt = rows(bv_all, c) * pinv
        kt = rows(k_all, c) * pinv
        bc.append((rows(bv_all, c) * prem).astype(BF16))
        kc.append((rows(k_all, c) * prem).astype(BF16))
        p_last.append(jnp.exp(cs_last))
        vst.append(_stack_heads(rows(v_all, c)).astype(BF16))
        lhs.append(jnp.concatenate([at[c], rt[c]], axis=0).astype(BF16))
        rhs.append(jnp.concatenate([_stack_heads(bt), _stack_heads(kt)], axis=0).astype(BF16))
    o = [_dot_nt(lhs[c], rhs[c]) for c in chunks]
    nmat = [jnp.where(strict, o[c][:L, :PAIR], 0.0) for c in chunks]
    akv = [_dot(jnp.where(strict, o[c][:L, PAIR:], 0.0).astype(BF16), vst[c]) for c in chunks]
    ident = jnp.where(tr == tc, 1.0, 0.0)
    tinv = [ident + jnp.where(off_diag(1), nmat[c], 0.0) for c in chunks]
    size = 2
    while size < L:
        mask = off_diag(size)
        tn = [_dot(tinv[c].astype(BF16), _stack_heads(jnp.where(mask, nmat[c], 0.0)).astype(BF16)) for c in chunks]
        tinv = [tinv[c] + _dot(tn[c].astype(BF16), _stack_heads(tinv[c]).astype(BF16)) for c in chunks]
        size *= 2
    z = [_dot(tinv[c].astype(BF16), _stack_heads(jnp.concatenate([at[c], akv[c]], axis=1)).astype(BF16))
         for c in chunks]
    tmp = [_dot(jnp.where(incl, o[c][L:, :PAIR], 0.0).astype(BF16), _stack_heads(z[c]).astype(BF16))
           for c in chunks]
    rkv = [_dot(jnp.where(incl, o[c][L:, PAIR:], 0.0).astype(BF16), vst[c]) for c in chunks]
    bz = [_dot_tn(bc[c], z[c].astype(BF16)) for c in chunks]
    kv = [_dot_tn(kc[c], rows(v_all, c).astype(BF16)) for c in chunks]
    state = s_ref[...]
    for c in chunks:
        sb = state.astype(BF16)
        rh = (rt[c] + tmp[c][:, :PAIR]).astype(BF16)
        yb_ref[c * L:(c + 1) * L, :] = _dot(rh, sb) + tmp[c][:, PAIR:] + rkv[c]
        m_c = jnp.where(eye, jnp.broadcast_to(p_last[c], (PAIR, PAIR)), 0.0) + jnp.where(same_head, bz[c][:, :PAIR], 0.0)
        state = _dot(m_c.astype(BF16), sb) + jnp.where(same_head, bz[c][:, PAIR:] + kv[c], 0.0)
    s_ref[...] = state

    y = yb_ref[...]
    bd_avg = bd_ones * (1.0 / HEAD)
    ym = _dot_sel(y, bd_avg)
    d = y - ym
    yv = _dot_sel(d * d, bd_avg)
    yn = d * lax.rsqrt(yv + GN_EPS) * lng_ref[...] + lnb_ref[...]
    bonus = _dot_sel(r_all * k_all * rk_ref[...], bd_ones) * v_all
    y_ref[0] = yn + bonus


def _wkv_scan(r, lw, k, v, kkr, a, r_k, lnx_g, lnx_b, tc):
    b, s, d = r.shape
    nchunk = tc // CHUNK
    tile = pl.BlockSpec((1, tc, PAIR), lambda i, p, j: (i, j, p))
    row = pl.BlockSpec((1, PAIR), lambda i, p, j: (0, p))
    return pl.pallas_call(
        functools.partial(_wkv_kernel, nchunk=nchunk),
        name="wkv_scan",
        grid=(b, d // PAIR, s // tc),
        in_specs=[tile] * 6 + [row] * 3,
        out_specs=tile,
        out_shape=jax.ShapeDtypeStruct((b, s, d), F32),
        scratch_shapes=[
            pltpu.VMEM((PAIR, PAIR), F32),
            pltpu.VMEM((tc, PAIR), F32),
        ],
        compiler_params=_params(("arbitrary", "arbitrary", "arbitrary")),
    )(r, lw, k, v, kkr, a, r_k.reshape(1, d), lnx_g.reshape(1, d), lnx_b.reshape(1, d))


def _mix_out_kernel(y_ref, g_ref, x_ref, mod_ref, wo_ref, ln_ref, o_ref):
    m = mod_ref[0]
    gate = m[2:3]
    yg = (y_ref[0] * g_ref[0].astype(F32)).astype(BF16)
    o = _dot(yg, wo_ref[...])
    z = ALPHA * x_ref[0] + gate * o
    ln = ln_ref[...]
    o_ref[0] = _layer_norm_rows(z, ln[0:1], ln[1:2])


def _mix_out(y, g, x, mod_l, w_o, ln_g, ln_b, ts):
    b, s, d = x.shape
    tile = pl.BlockSpec((1, ts, d), lambda i, j: (i, j, 0))
    ln = jnp.zeros((8, d), F32).at[0].set(ln_g).at[1].set(ln_b)
    return pl.pallas_call(
        _mix_out_kernel,
        name="mix_out",
        grid=(b, s // ts),
        in_specs=[tile, tile, tile,
                  pl.BlockSpec((1, 6, d), lambda i, j: (i, 0, 0)),
                  pl.BlockSpec((d, d), lambda i, j: (0, 0)),
                  pl.BlockSpec((8, d), lambda i, j: (0, 0))],
        out_specs=tile,
        out_shape=jax.ShapeDtypeStruct((b, s, d), F32),
        compiler_params=_params(("arbitrary", "arbitrary")),
    )(y, g, x, mod_l, w_o.astype(BF16), ln)


def _pool_kernel(x_ref, mod_ref, win_ref, wgrp_ref, sc_ref, wout_ref, ln_ref, o_ref, zext_ref, *, ts):
    s = pl.program_id(1)
    m = mod_ref[0]
    shift, scale, gate = m[0:1], m[1:2], m[2:3]
    x = x_ref[0]
    h = x * (1.0 + scale) + shift
    z = _dot(h.astype(BF16), win_ref[...])

    @pl.when(s == 0)
    def _():
        zext_ref[0:POOL_HALO, :] = jnp.zeros((POOL_HALO, z.shape[1]), F32)

    zext_ref[POOL_HALO:POOL_HALO + ts, :] = z
    pos = s * ts + lax.broadcasted_iota(jnp.int32, (ts, 1), 0)
    gw = z.shape[1] // len(POOL_WINDOWS)
    parts = []
    for gi, win in enumerate(POOL_WINDOWS):
        lo = gi * gw
        zg = z[:, lo:lo + gw]
        acc = zg
        for back in range(1, win):
            acc = acc + zext_ref[POOL_HALO - back:POOL_HALO - back + ts, lo:lo + gw]
        cnt = jnp.minimum(pos + 1, win).astype(F32)
        p = acc / cnt - zg
        parts.append(_dot(p.astype(BF16), wgrp_ref[gi]))
    y = jnp.concatenate(parts, axis=1) * sc_ref[...]
    o = _dot(y.astype(BF16), wout_ref[...])
    zext_ref[0:POOL_HALO, :] = zext_ref[ts:ts + POOL_HALO, :]
    ln = ln_ref[...]
    o_ref[0] = _layer_norm_rows(ALPHA * x + gate * o, ln[0:1], ln[1:2])


def _pool_mix(x, mod_l, w_in, w_grp, scale, w_out, ln_g, ln_b, ts):
    b, s, d = x.shape
    ng, gw, _ = w_grp.shape
    tile = pl.BlockSpec((1, ts, d), lambda i, j: (i, j, 0))
    ln = jnp.zeros((8, d), F32).at[0].set(ln_g).at[1].set(ln_b)
    return pl.pallas_call(
        functools.partial(_pool_kernel, ts=ts),
        name="pool_mix",
        grid=(b, s // ts),
        in_specs=[tile,
                  pl.BlockSpec((1, 6, d), lambda i, j: (i, 0, 0)),
                  pl.BlockSpec((d, d), lambda i, j: (0, 0)),
                  pl.BlockSpec((ng, gw, gw), lambda i, j: (0, 0, 0)),
                  pl.BlockSpec((1, d), lambda i, j: (0, 0)),
                  pl.BlockSpec((d, d), lambda i, j: (0, 0)),
                  pl.BlockSpec((8, d), lambda i, j: (0, 0))],
        out_specs=tile,
        out_shape=jax.ShapeDtypeStruct((b, s, d), F32),
        scratch_shapes=[pltpu.VMEM((ts + POOL_HALO, d), F32)],
        compiler_params=_params(("arbitrary", "arbitrary")),
    )(x, mod_l, w_in.astype(BF16), w_grp.astype(BF16), scale.reshape(1, d), w_out.astype(BF16), ln)


def _fold_kernel(keys_ref, wq_ref, o_ref):
    o_ref[0] = lax.dot_general(keys_ref[0, 0], wq_ref[0], NT_DIMS, precision=HI,
                               preferred_element_type=F32).astype(BF16)


def _peer_fold(keys, w_q):
    depth, nh, two, nk, dk = keys.shape
    d = w_q.shape[1]
    nb = nh * two
    return pl.pallas_call(
        _fold_kernel,
        name="peer_fold",
        grid=(depth, nb),
        in_specs=[pl.BlockSpec((1, 1, nk, dk), lambda l, j: (l, j, 0, 0)),
                  pl.BlockSpec((1, d, dk), lambda l, j: (l, 0, j))],
        out_specs=pl.BlockSpec((1, nk, d), lambda l, j: (l, j, 0)),
        out_shape=jax.ShapeDtypeStruct((depth, nb * nk, d), BF16),
        compiler_params=_params(("arbitrary", "arbitrary")),
    )(keys.reshape(depth, nb, nk, dk), w_q)


N_TOP = PEER_TOPK + 1
CAND = [(a, b) for a in range(N_TOP) for b in range(N_TOP) if (a + 1) * (b + 1) <= N_TOP]
CAND_ROWS = 1 << (len(CAND) - 1).bit_length()


def _sorting_network(n):
    pairs = []

    def merge(lo, cnt, step):
        nxt = step * 2
        if nxt < cnt:
            merge(lo, cnt, nxt)
            merge(lo + step, cnt, nxt)
            for i in range(lo + step, lo + cnt - step, nxt):
                pairs.append((i, i + step))
        else:
            pairs.append((lo, lo + step))

    def sort(lo, cnt):
        if cnt > 1:
            half = cnt // 2
            sort(lo, half)
            sort(lo + half, half)
            merge(lo, cnt, 1)

    sort(0, n)
    return pairs


def _top_sorted(s, n):
    if s.shape[1] > LANES:
        parts = [_top_sorted(s[:, c:c + LANES], n) for c in range(0, s.shape[1], LANES)]
        return [jnp.concatenate([p[k] for p in parts], axis=1) for k in range(n)]
    groups = s.shape[0] // SUB
    cols = [s[g * SUB:(g + 1) * SUB, :] for g in range(groups)]
    for i, j in _sorting_network(groups):
        cols[i], cols[j] = jnp.maximum(cols[i], cols[j]), jnp.minimum(cols[i], cols[j])
    sub = lax.broadcasted_iota(jnp.int32, cols[0].shape, 0)
    outs = []
    for it in range(n):
        mx = jnp.max(cols[0], axis=0, keepdims=True)
        outs.append(mx)
        left = n - 1 - it
        if left == 0:
            break
        first = jnp.min(jnp.where(cols[0] == mx, sub, SUB), axis=0, keepdims=True)
        pop = sub == first
        for lvl in range(min(groups, left)):
            below = cols[lvl + 1] if lvl + 1 < groups else NEG_INF
            cols[lvl] = jnp.where(pop, below, cols[lvl])
    return outs


def _twin_bf16_words(x):
    bits = pltpu.bitcast(x.astype(BF16).astype(F32), jnp.uint32)
    return bits | (bits >> 16)


def _route_kernel(x_ref, mod_ref, wf_ref, h2_ref, e2_ref, rk_ref, c1_ref, n1_ref):
    hd = pl.program_id(1)
    m = mod_ref[0]
    shift, scale = m[3:4], m[4:5]

    @pl.when(hd == 0)
    def _():
        h2_ref[...] = (x_ref[...] * (1.0 + scale) + shift).T.astype(BF16)

    h2 = h2_ref[...]
    s1 = _dot(wf_ref[0:N_KEYS, :], h2)
    s2 = _dot(wf_ref[N_KEYS:2 * N_KEYS, :], h2)
    ta = _top_sorted(s1, N_TOP)
    tb = _top_sorted(s2, N_TOP)
    rows = [ta[a] + tb[b] for a, b in CAND]
    rows += [jnp.full_like(rows[0], NEG_INF)] * (CAND_ROWS - len(CAND))
    cand = jnp.concatenate(rows, axis=0)
    tc = _top_sorted(cand, N_TOP)
    tau = 0.5 * (tc[PEER_TOPK - 1] + tc[PEER_TOPK])
    top = ta[0] + tb[0]
    zsum = jnp.sum(jnp.where(cand >= tau, jnp.exp(cand - top), 0.0), axis=0, keepdims=True)
    need = tau - s1
    rank2 = jnp.zeros_like(s2)
    count1 = jnp.zeros_like(s1)
    for b in range(PEER_TOPK):
        rank2 = jnp.where(tb[b] > s2, b + 1.0, rank2)
        count1 = jnp.where(tb[b] >= need, b + 1.0, count1)
    e2_ref[0] = jnp.exp(s2 - tb[0]).astype(BF16)
    rk_ref[0] = rank2.astype(BF16)
    c1w = _twin_bf16_words(jnp.exp(s1 - ta[0]) / zsum)
    n1w = _twin_bf16_words(count1)
    for blk in range(c1_ref.shape[1]):
        c1_ref[0, blk] = c1w[:, blk * LANES:(blk + 1) * LANES]
        n1_ref[0, blk] = n1w[:, blk * LANES:(blk + 1) * LANES]


def _peer_route(x2d, mod_l, wf, tt, tiles_per_batch):
    t, d = x2d.shape
    nh = PEER_HEADS
    per_head = pl.BlockSpec((1, N_KEYS, tt), lambda i, h: (h, 0, i))
    per_key = pl.BlockSpec((1, tt // LANES, N_KEYS, LANES), lambda i, h: (h, i, 0, 0))
    return pl.pallas_call(
        _route_kernel,
        name="peer_route",
        grid=(t // tt, nh),
        in_specs=[pl.BlockSpec((tt, d), lambda i, h: (i, 0)),
                  pl.BlockSpec((1, 6, d), lambda i, h: (i // tiles_per_batch, 0, 0)),
                  pl.BlockSpec((2 * N_KEYS, d), lambda i, h: (h, 0))],
        out_specs=[pl.BlockSpec((d, tt), lambda i, h: (0, i)), per_head, per_head, per_key, per_key],
        out_shape=[jax.ShapeDtypeStruct((d, t), BF16),
                   jax.ShapeDtypeStruct((nh, N_KEYS, t), BF16),
                   jax.ShapeDtypeStruct((nh, N_KEYS, t), BF16),
                   jax.ShapeDtypeStruct((nh, t // LANES, N_KEYS, LANES), jnp.uint32),
                   jax.ShapeDtypeStruct((nh, t // LANES, N_KEYS, LANES), jnp.uint32)],
        compiler_params=_params(("arbitrary", "arbitrary")),
    )(x2d, mod_l, wf)


ROWS = 16
TOKEN_COLS = 256
MXU_ROWS = 256
MXU_COLS = 256


def _dense_kernel(h2_ref, e2_ref, rk_ref, c1_ref, n1_ref, u_ref, vt_ref, x_ref, mod_ref, ln_ref, o_ref,
                  acc_ref, za_ref, zb_ref, pa_ref, pb_ref, *, eb):
    e = pl.program_id(1)
    last = pl.num_programs(1) - 1
    tt = h2_ref.shape[1]
    slabs = eb // N_KEYS

    @pl.when(e == 0)
    def _():
        acc_ref[...] = jnp.zeros_like(acc_ref)
        zb_ref[...] = jnp.zeros_like(zb_ref)
        pa_ref[...] = jnp.zeros_like(pa_ref)

    def gate_blocks(z_ref, p_ref):
        blocks = []
        for il in range(slabs):
            i1 = jnp.clip((e - 1) * slabs + il, 0, N_KEYS - 1)
            for t0 in range(0, tt, TOKEN_COLS):
                cols = slice(t0, t0 + TOKEN_COLS)
                rows_of = {}

                def token_rows(i1=i1, t0=t0, rows_of=rows_of):
                    if not rows_of:
                        def tile_of(ref, hd):
                            words = [jnp.broadcast_to(ref[hd, tb, pl.ds(i1, 1), :], (ROWS // 2, LANES))
                                     for tb in range(t0 // LANES, (t0 + TOKEN_COLS) // LANES)]
                            return pltpu.bitcast(jnp.concatenate(words, axis=1), BF16)
                        rows_of["cnt"] = [tile_of(n1_ref, hd) for hd in range(PEER_HEADS)]
                        rows_of["wgt"] = [tile_of(c1_ref, hd) for hd in range(PEER_HEADS)]
                    return rows_of["cnt"], rows_of["wgt"]

                for jb in range(N_KEYS // ROWS):
                    def block(il=il, jb=jb, cols=cols, token_rows=token_rows):
                        cnt, wgt = token_rows()
                        r0 = il * N_KEYS + jb * ROWS
                        zz = z_ref[r0:r0 + ROWS, cols]
                        act = (0.5 * zz * (1.0 + lax.erf(zz * INV_SQRT2))).astype(BF16)
                        gsum = jnp.zeros((ROWS, TOKEN_COLS), BF16)
                        for hd in range(PEER_HEADS):
                            keep = rk_ref[hd, jb * ROWS:(jb + 1) * ROWS, cols] < cnt[hd]
                            gsum = gsum + jnp.where(keep, e2_ref[hd, jb * ROWS:(jb + 1) * ROWS, cols], 0.0) * wgt[hd]
                        p_ref[r0:r0 + ROWS, cols] = act * gsum
                    blocks.append(block)
        return blocks

    def matmul_units(z_new, p_old):
        units = []
        for n0 in range(0, tt, MXU_COLS):
            for m0 in range(0, acc_ref.shape[0], MXU_ROWS):
                def unit(m0=m0, n0=n0):
                    acc_ref[m0:m0 + MXU_ROWS, n0:n0 + MXU_COLS] += _dot(vt_ref[0, m0:m0 + MXU_ROWS, :], p_old[:, n0:n0 + MXU_COLS])
                units.append((eb, unit))
            for m0 in range(0, eb, MXU_ROWS):
                def unit(m0=m0, n0=n0):
                    z_new[m0:m0 + MXU_ROWS, n0:n0 + MXU_COLS] = _dot(u_ref[m0:m0 + MXU_ROWS, :], h2_ref[:, n0:n0 + MXU_COLS])
                units.append((u_ref.shape[1], unit))
        return units

    def step(z_new, z_old, p_new, p_old):
        blocks = gate_blocks(z_old, p_new)
        units = matmul_units(z_new, p_old)
        total = sum(k for k, _ in units)
        done = 0
        issued = 0
        for k, unit in units:
            unit()
            done += k
            upto = -(-len(blocks) * done // total)
            for block in blocks[issued:upto]:
                block()
            issued = upto

    @pl.when(e % 2 == 0)
    def _():
        step(za_ref, zb_ref, pb_ref, pa_ref)

    @pl.when(e % 2 == 1)
    def _():
        step(zb_ref, za_ref, pa_ref, pb_ref)

    @pl.when(e == last)
    def _():
        m = mod_ref[0]
        gate = m[5:6]
        ln = ln_ref[...]
        y = acc_ref[...].T
        o_ref[...] = _layer_norm_rows(ALPHA * x_ref[...] + gate * y, ln[0:1], ln[1:2])


def _peer_dense(h2, e2, rk2, c1, n1, u_bf, vt_bf, x2d, mod_l, ln_g, ln_b, tt, eb, tiles_per_batch):
    t, d = x2d.shape
    nh = PEER_HEADS
    ln = jnp.zeros((8, d), F32).at[0].set(ln_g).at[1].set(ln_b)
    nslab = u_bf.shape[0] // eb
    per_tile = pl.BlockSpec((nh, N_KEYS, tt), lambda i, e: (0, 0, i))
    per_key = pl.BlockSpec((nh, tt // LANES, N_KEYS, LANES), lambda i, e: (0, i, 0, 0))
    return pl.pallas_call(
        functools.partial(_dense_kernel, eb=eb),
        name="peer_dense",
        grid=(t // tt, nslab + 2),
        in_specs=[pl.BlockSpec((d, tt), lambda i, e: (0, i)),
                  per_tile, per_tile, per_key, per_key,
                  pl.BlockSpec((eb, d), lambda i, e: (jnp.minimum(e, nslab - 1), 0)),
                  pl.BlockSpec((1, d, eb), lambda i, e: (jnp.clip(e - 2, 0, nslab - 1), 0, 0)),
                  pl.BlockSpec((tt, d), lambda i, e: (i, 0)),
                  pl.BlockSpec((1, 6, d), lambda i, e: (i // tiles_per_batch, 0, 0)),
                  pl.BlockSpec((8, d), lambda i, e: (0, 0))],
        out_specs=pl.BlockSpec((tt, d), lambda i, e: (i, 0)),
        out_shape=jax.ShapeDtypeStruct((t, d), F32),
        scratch_shapes=[pltpu.VMEM((d, tt), F32),
                        pltpu.VMEM((eb, tt), F32),
                        pltpu.VMEM((eb, tt), F32),
                        pltpu.VMEM((eb, tt), BF16),
                        pltpu.VMEM((eb, tt), BF16)],
        compiler_params=_params(("arbitrary", "arbitrary")),
    )(h2, e2, rk2, c1, n1, u_bf, vt_bf, x2d, mod_l, ln)


def _peer_ffn(x, mod_l, wf, u_tab, v_tab, ln_g, ln_b, tt_route, tt_dense, eb):
    b, s, d = x.shape
    x2d = x.reshape(b * s, d)
    h2, e2, rk2, c1, n1 = _peer_route(x2d, mod_l, wf, tt_route, s // tt_route)
    vt = v_tab.astype(BF16).reshape(v_tab.shape[0] // eb, eb, d).transpose(0, 2, 1)
    out = _peer_dense(h2, e2, rk2, c1, n1, u_tab.astype(BF16), vt, x2d, mod_l, ln_g, ln_b,
                      tt_dense, eb, s // tt_dense)
    return out.reshape(b, s, d)


def kernel(x, c, ada_w, ada_b, ln_g, ln_b, rw_mu, rw_w_rkv, rw_w0, rw_w1, rw_w2, rw_a0, rw_a1, rw_a2, rw_g1, rw_g2, rw_k_k, rw_k_a, rw_r_k, rw_lnx_g, rw_lnx_b, rw_w_o, pl_w_in, pl_w_grp, pl_scale, pl_w_out, pe_w_q, pe_keys, pe_u, pe_v):
    b, s, d = x.shape
    depth = ada_w.shape[0]
    ts = min(256, s)
    tc = min(1024, s)
    tt = min(512, s)
    tt_dense = min(512, s)
    eb = 512
    mod = _adaln_mod(c, ada_w, ada_b)
    wf = _peer_fold(pe_keys, pe_w_q)
    for i in range(depth):
        j = i // 2
        if i % 2 == 0:
            r, lw, k, v, kkr, a, g = _rwkv_proj(x, mod[i], rw_mu[j], rw_w_rkv[j], rw_w0[j], rw_w1[j], rw_w2[j],
                                                rw_a0[j], rw_a1[j], rw_a2[j], rw_g1[j], rw_g2[j],
                                                rw_k_k[j], rw_k_a[j], ts)
            y = _wkv_scan(r, lw, k, v, kkr, a, rw_r_k[j], rw_lnx_g[j], rw_lnx_b[j], tc)
            x = _mix_out(y, g, x, mod[i], rw_w_o[j], ln_g[i, 0], ln_b[i, 0], ts)
        else:
            x = _pool_mix(x, mod[i], pl_w_in[j], pl_w_grp[j], pl_scale[j], pl_w_out[j],
                          ln_g[i, 0], ln_b[i, 0], ts)
        x = _peer_ffn(x, mod[i], wf[i], pe_u[i], pe_v[i], ln_g[i, 1], ln_b[i, 1], tt, tt_dense, eb)
    return x
```
